```python
import math
import jax, jax.numpy as jnp
from jax import lax
import numpy as np

D_MODEL = 4096
BATCH = 1
SEQ = 8192
DEPTH = 1
DEC_BATCH = 32
DEC_SEQ = 1
PAST_LEN = 8192
PAGE_SIZE = 128

MOBA_HEADS = 16
MOBA_HEAD_DIM = 128
MOBA_BLOCK = 256
MOBA_TOPK = 3
MOBA_QCHUNK = 32
MOBA_WIDTH = MOBA_HEADS * MOBA_HEAD_DIM
GMLP_GROUPS = 8
GMLP_GROUP_DIM = 128
GMLP_CHUNK = 128
GMLP_WIDTH = GMLP_GROUPS * GMLP_GROUP_DIM
MEM_TOKENS = 256
MEM_HEADS = 4
MEM_HEAD_DIM = 256
MEM_WIDTH = MEM_HEADS * MEM_HEAD_DIM
REL_BUCKETS = 32
REL_MAX_DISTANCE = 128
N_EXPERTS = 32
TOP_K = 4
D_FF = D_MODEL
SWIGLU_LIMIT = 7.0
SWIGLU_ALPHA = 1.702
N_BRANCHES = 3
IN_WIDTH = 3 * MOBA_WIDTH + 2 * GMLP_WIDTH + MEM_WIDTH + N_BRANCHES * D_MODEL
NORM_EPS = 1e-6

kernel_name = 'hybrid_moba_gmlp_memxattn_moe_step'


def rmsnorm(x, g):
    xf = x.astype(jnp.float32)
    y = xf * lax.rsqrt(jnp.mean(xf * xf, axis=-1, keepdims=True) + NORM_EPS)
    return (y * g.astype(jnp.float32)).astype(x.dtype)


def layernorm(x, g, b):
    xf = x.astype(jnp.float32)
    mu = jnp.mean(xf, axis=-1, keepdims=True)
    var = jnp.mean(jnp.square(xf - mu), axis=-1, keepdims=True)
    y = (xf - mu) * lax.rsqrt(var + NORM_EPS) * g.astype(jnp.float32) + b.astype(jnp.float32)
    return y.astype(x.dtype)


def t5_bucket(dist):
    n = jnp.maximum(dist, 0)
    max_exact = REL_BUCKETS // 2
    log_ratio = jnp.log(jnp.maximum(n, 1).astype(jnp.float32) / max_exact) / math.log(REL_MAX_DISTANCE / max_exact)
    large = max_exact + (log_ratio * (REL_BUCKETS - max_exact)).astype(jnp.int32)
    return jnp.where(n < max_exact, n, jnp.minimum(large, REL_BUCKETS - 1))


def pad_axis1(a, block):
    L = a.shape[1]
    pad = (-L) % block
    return jnp.pad(a, [(0, 0), (0, pad)] + [(0, 0)] * (a.ndim - 2))


def block_mean(k_pad):
    B, L, H, hd = k_pad.shape
    return k_pad.reshape(B, L // MOBA_BLOCK, MOBA_BLOCK, H, hd).astype(jnp.float32).mean(axis=2)


def gather_rows(src, pos):
    b = jnp.arange(src.shape[0])[:, None, None, None, None]
    h = jnp.arange(src.shape[2])[None, None, :, None, None]
    return src[b, pos, h]


def moba_attend(q, q_pos, k_means, k_src, v_rows, rel_bias):
    B, Tq, H, hd = q.shape
    nb = k_means.shape[1]
    own = q_pos // MOBA_BLOCK
    gate = jnp.einsum('bqhd,bnhd->bqhn', q, k_means.astype(q.dtype), preferred_element_type=jnp.float32)
    fully_past = jnp.arange(nb)[None, :] < own[:, None]
    gate = jnp.where(fully_past[None, :, None, :], gate, -jnp.inf)
    n_sel = min(MOBA_TOPK, nb)
    _, sel = lax.top_k(gate, n_sel)
    own_b = jnp.broadcast_to(own[None, :, None, None], (B, Tq, H, 1)).astype(sel.dtype)
    blocks = jnp.concatenate([sel, own_b], axis=-1)
    slot_ok = jnp.concatenate([jnp.arange(n_sel)[None, :] < own[:, None], jnp.ones((Tq, 1), bool)], axis=-1)
    kpos = blocks[..., None] * MOBA_BLOCK + jnp.arange(MOBA_BLOCK)
    k_sel = gather_rows(k_src, kpos)
    v_sel = v_rows(kpos)
    dist = q_pos[None, :, None, None, None] - kpos
    h_idx = jnp.arange(H)[None, None, :, None, None]
    bias = rel_bias[t5_bucket(dist), h_idx].astype(jnp.float32)
    logits = jnp.einsum('bqhd,bqhskd->bqhsk', q, k_sel, preferred_element_type=jnp.float32) * (hd ** -0.5) + bias
    mask = slot_ok[None, :, None, :, None] & (dist >= 0)
    logits = jnp.where(mask, logits, -jnp.inf)
    p = jax.nn.softmax(logits.reshape(B, Tq, H, -1), axis=-1).reshape(logits.shape)
    out = jnp.einsum('bqhsk,bqhskd->bqhd', p.astype(v_sel.dtype), v_sel, preferred_element_type=jnp.float32)
    return out.astype(q.dtype)


def gmlp_mix(v, w_s, b_s):
    B, T, G, dg = v.shape
    vp = pad_axis1(v, GMLP_CHUNK)
    nc = vp.shape[1] // GMLP_CHUNK
    vp = vp.reshape(B, nc, GMLP_CHUNK, G, dg)
    tril = jnp.tril(jnp.ones((GMLP_CHUNK, GMLP_CHUNK), dtype=bool))
    ws = jnp.where(tril, w_s, jnp.zeros_like(w_s))
    mixed = jnp.einsum('gij,bcjgd->bcigd', ws, vp) + jnp.swapaxes(b_s, 0, 1)[:, :, None]
    return mixed.reshape(B, nc * GMLP_CHUNK, G, dg)[:, :T]


def mem_attend(q, k, v):
    logits = jnp.einsum('bqhd,bmhd->bhqm', q, k, preferred_element_type=jnp.float32) * (q.shape[-1] ** -0.5)
    p = jax.nn.softmax(logits, axis=-1)
    out = jnp.einsum('bhqm,bmhd->bqhd', p.astype(v.dtype), v, preferred_element_type=jnp.float32)
    return out.astype(q.dtype)


def moe(h, w_router, b_router, w_gate_up, b_gate_up, w_down, b_down):
    logits = jnp.dot(h, w_router, preferred_element_type=jnp.float32) + b_router.astype(jnp.float32)
    top_v, top_i = lax.top_k(logits, TOP_K)
    top_w = jax.nn.softmax(top_v, axis=-1)
    gates = jnp.sum(jax.nn.one_hot(top_i, N_EXPERTS, dtype=jnp.float32) * top_w[..., None], axis=-2).astype(h.dtype)
    out = jnp.zeros_like(h)
    for e in range(N_EXPERTS):
        gu = h @ w_gate_up[e] + b_gate_up[e]
        glu = jnp.minimum(gu[:, 0::2], SWIGLU_LIMIT)
        lin = jnp.clip(gu[:, 1::2], -SWIGLU_LIMIT, SWIGLU_LIMIT)
        act = glu * jax.nn.sigmoid(SWIGLU_ALPHA * glu) * (lin + 1.0)
        out = out + gates[:, e:e + 1] * (act @ w_down[e] + b_down[e])
    return out


def _layer(x_all, B, S, DB, DS, mem_prompt, cache_k, cache_v, cache_mem_k, cache_mem_v, page_table, rel_bias,
           norm1_g, w_in, q_norm_g, k_norm_g, gmlp_ln_g, gmlp_ln_b, gmlp_ws, gmlp_bs, mem_norm_g, w_mem_kv,
           mq_norm_g, mk_norm_g, w_br_moba, w_br_gmlp, w_br_mem, w_out, norm2_g, w_router, b_router,
           w_gate_up, b_gate_up, w_down, b_down):
    n_p = B * S

    def groups(a):
        return a[:n_p].reshape((B, S) + a.shape[1:]), a[n_p:].reshape((DB, DS) + a.shape[1:])

    xn = rmsnorm(x_all, norm1_g)
    proj = xn @ w_in
    offs = [int(o) for o in np.cumsum([MOBA_WIDTH, MOBA_WIDTH, MOBA_WIDTH, 2 * GMLP_WIDTH, MEM_WIDTH])]
    q, k, v, z, mq, gl = jnp.split(proj, offs, axis=-1)
    q = rmsnorm(q.reshape(-1, MOBA_HEADS, MOBA_HEAD_DIM), q_norm_g)
    k = rmsnorm(k.reshape(-1, MOBA_HEADS, MOBA_HEAD_DIM), k_norm_g)
    v = v.reshape(-1, MOBA_HEADS, MOBA_HEAD_DIM)
    z = jax.nn.gelu(z, approximate=False)
    u, gv = jnp.split(z, 2, axis=-1)
    gv = layernorm(gv, gmlp_ln_g, gmlp_ln_b).reshape(-1, GMLP_GROUPS, GMLP_GROUP_DIM)
    mq = rmsnorm(mq.reshape(-1, MEM_HEADS, MEM_HEAD_DIM), mq_norm_g)
    gates = jax.nn.sigmoid(gl.astype(jnp.float32)).astype(x_all.dtype).reshape(-1, N_BRANCHES, D_MODEL)

    q_p, q_s = groups(q)
    k_p, k_s = groups(k)
    v_p, v_s = groups(v)
    gv_p, gv_s = groups(gv)
    mq_p, mq_s = groups(mq)

    k_pad = pad_axis1(k_p, MOBA_BLOCK)
    v_pad = pad_axis1(v_p, MOBA_BLOCK)
    k_means = block_mean(k_pad)
    n_qc = S // MOBA_QCHUNK
    q_chunks = jnp.moveaxis(q_p.reshape(B, n_qc, MOBA_QCHUNK, MOBA_HEADS, MOBA_HEAD_DIM), 1, 0)
    pos_chunks = jnp.arange(S, dtype=jnp.int32).reshape(n_qc, MOBA_QCHUNK)
    att_p = lax.map(lambda c: moba_attend(c[0], c[1], k_means, k_pad, lambda kp: gather_rows(v_pad, kp), rel_bias),
                    (q_chunks, pos_chunks))
    att_p = jnp.moveaxis(att_p, 0, 1).reshape(n_p, MOBA_WIDTH)

    page_size = cache_k.shape[1]
    past_len = page_table.shape[1] * page_size
    k_past = cache_k[page_table].reshape(DB, past_len, MOBA_HEADS, MOBA_HEAD_DIM)
    ks_pad = pad_axis1(jnp.concatenate([k_past, k_s.astype(k_past.dtype)], axis=1), MOBA_BLOCK)
    ks_means = block_mean(ks_pad)
    b_idx = jnp.arange(DB)[:, None, None, None, None]
    h_idx = jnp.arange(MOBA_HEADS)[None, None, :, None, None]

    def v_rows_sample(kpos):
        past = jnp.minimum(kpos, past_len - 1)
        v_old = cache_v[page_table[b_idx, past // page_size], past % page_size, h_idx]
        v_new = v_s[b_idx, jnp.clip(kpos - past_len, 0, DS - 1), h_idx].astype(v_old.dtype)
        return jnp.where((kpos >= past_len)[..., None], v_new, v_old)

    q_pos_s = past_len + jnp.arange(DS, dtype=jnp.int32)
    att_s = moba_attend(q_s, q_pos_s, ks_means, ks_pad, v_rows_sample, rel_bias).reshape(DB * DS, MOBA_WIDTH)
    att = jnp.concatenate([att_p, att_s], axis=0)

    sgu = jnp.concatenate([gmlp_mix(gv_p, gmlp_ws, gmlp_bs).reshape(n_p, GMLP_WIDTH),
                           gmlp_mix(gv_s, gmlp_ws, gmlp_bs).reshape(DB * DS, GMLP_WIDTH)], axis=0)
    g_out = u * sgu

    mem_n = rmsnorm(mem_prompt, mem_norm_g)
    mk_p, mv_p = jnp.split(mem_n @ w_mem_kv, 2, axis=-1)
    M = mem_prompt.shape[1]
    mk_p = rmsnorm(mk_p.reshape(B, M, MEM_HEADS, MEM_HEAD_DIM), mk_norm_g)
    mv_p = mv_p.reshape(B, M, MEM_HEADS, MEM_HEAD_DIM)
    m_out = jnp.concatenate([mem_attend(mq_p, mk_p, mv_p).reshape(n_p, MEM_WIDTH),
                             mem_attend(mq_s, cache_mem_k, cache_mem_v).reshape(DB * DS, MEM_WIDTH)], axis=0)

    mixed = (gates[:, 0] * (att @ w_br_moba) + gates[:, 1] * (g_out @ w_br_gmlp)
             + gates[:, 2] * (m_out @ w_br_mem))
    h = x_all + mixed @ w_out
    out = h + moe(rmsnorm(h, norm2_g), w_router, b_router, w_gate_up, b_gate_up, w_down, b_down)
    return out, (k_p, v_p, mk_p, mv_p, k_s, v_s, gv_s)


def setup_inputs(seed: int = 0) -> dict:
    key = jax.random.key(seed)
    ks = jax.random.split(key, 40)
    f32 = jnp.float32

    def nrm(k, shape, scale):
        return jax.random.normal(k, shape, f32) * scale

    def gain(k, shape):
        return 1.0 + 0.05 * jax.random.normal(k, shape, f32)

    n_pages = PAST_LEN // PAGE_SIZE
    n_used = DEC_BATCH * n_pages
    n_phys = n_used + -(-n_used // 4)
    perm = jax.random.permutation(ks[0], n_phys)
    page_table = perm[:n_used].reshape(DEC_BATCH, n_pages).astype(jnp.int32)
    D = D_MODEL
    return {
        'x_prompt': nrm(ks[1], (BATCH, SEQ, D), 1.0),
        'x_sample': nrm(ks[2], (DEC_BATCH, DEC_SEQ, D), 1.0),
        'cache_k': nrm(ks[3], (DEPTH, n_phys, PAGE_SIZE, MOBA_HEADS, MOBA_HEAD_DIM), 1.0),
        'cache_v': nrm(ks[4], (DEPTH, n_phys, PAGE_SIZE, MOBA_HEADS, MOBA_HEAD_DIM), 1.0),
        'cache_mem_k': nrm(ks[5], (DEPTH, DEC_BATCH, MEM_TOKENS, MEM_HEADS, MEM_HEAD_DIM), 1.0),
        'cache_mem_v': nrm(ks[6], (DEPTH, DEC_BATCH, MEM_TOKENS, MEM_HEADS, MEM_HEAD_DIM), 1.0),
        'page_table': page_table,
        'mem_prompt': nrm(ks[7], (BATCH, MEM_TOKENS, D), 1.0),
        'rel_bias': nrm(ks[8], (REL_BUCKETS, MOBA_HEADS), 0.5),
        'norm1_g': gain(ks[9], (DEPTH, D)),
        'w_in': nrm(ks[10], (DEPTH, D, IN_WIDTH), D ** -0.5),
        'q_norm_g': gain(ks[11], (DEPTH, MOBA_HEAD_DIM)),
        'k_norm_g': gain(ks[12], (DEPTH, MOBA_HEAD_DIM)),
        'gmlp_ln_g': gain(ks[13], (DEPTH, GMLP_WIDTH)),
        'gmlp_ln_b': nrm(ks[14], (DEPTH, GMLP_WIDTH), 0.02),
        'gmlp_ws': nrm(ks[15], (DEPTH, GMLP_GROUPS, GMLP_CHUNK, GMLP_CHUNK), GMLP_CHUNK ** -0.5),
        'gmlp_bs': 1.0 + nrm(ks[16], (DEPTH, GMLP_GROUPS, GMLP_CHUNK), 0.1),
        'mem_norm_g': gain(ks[17], (DEPTH, D)),
        'w_mem_kv': nrm(ks[18], (DEPTH, D, 2 * MEM_WIDTH), D ** -0.5),
        'mq_norm_g': gain(ks[19], (DEPTH, MEM_HEAD_DIM)),
        'mk_norm_g': gain(ks[20], (DEPTH, MEM_HEAD_DIM)),
        'w_br_moba': nrm(ks[21], (DEPTH, MOBA_WIDTH, D), MOBA_WIDTH ** -0.5),
        'w_br_gmlp': nrm(ks[22], (DEPTH, GMLP_WIDTH, D), GMLP_WIDTH ** -0.5),
        'w_br_mem': nrm(ks[23], (DEPTH, MEM_WIDTH, D), MEM_WIDTH ** -0.5),
        'w_out': nrm(ks[24], (DEPTH, D, D), D ** -0.5),
        'norm2_g': gain(ks[25], (DEPTH, D)),
        'w_router': nrm(ks[26], (DEPTH, D, N_EXPERTS), D ** -0.5),
        'b_router': nrm(ks[27], (DEPTH, N_EXPERTS), 0.01),
        'w_gate_up': nrm(ks[28], (DEPTH, N_EXPERTS, D, 2 * D_FF), D ** -0.5),
        'b_gate_up': nrm(ks[29], (DEPTH, N_EXPERTS, 2 * D_FF), 0.01),
        'w_down': nrm(ks[30], (DEPTH, N_EXPERTS, D_FF, D), D_FF ** -0.5),
        'b_down': nrm(ks[31], (DEPTH, N_EXPERTS, D), 0.01),
    }


def reference(x_prompt, x_sample, cache_k, cache_v, cache_mem_k, cache_mem_v, page_table, mem_prompt,
              rel_bias, norm1_g, w_in, q_norm_g, k_norm_g, gmlp_ln_g, gmlp_ln_b, gmlp_ws, gmlp_bs,
              mem_norm_g, w_mem_kv, mq_norm_g, mk_norm_g, w_br_moba, w_br_gmlp, w_br_mem, w_out,
              norm2_g, w_router, b_router, w_gate_up, b_gate_up, w_down, b_down):
    B, S, D = x_prompt.shape
    DB, DS, _ = x_sample.shape
    x_all = jnp.concatenate([x_prompt.reshape(B * S, D), x_sample.reshape(DB * DS, D)], axis=0)
    states = []
    for l in range(DEPTH):
        x_all, st = _layer(x_all, B, S, DB, DS, mem_prompt, cache_k[l], cache_v[l], cache_mem_k[l],
                           cache_mem_v[l], page_table, rel_bias, norm1_g[l], w_in[l], q_norm_g[l],
                           k_norm_g[l], gmlp_ln_g[l], gmlp_ln_b[l], gmlp_ws[l], gmlp_bs[l], mem_norm_g[l],
                           w_mem_kv[l], mq_norm_g[l], mk_norm_g[l], w_br_moba[l], w_br_gmlp[l], w_br_mem[l],
                           w_out[l], norm2_g[l], w_router[l], b_router[l], w_gate_up[l], b_gate_up[l],
                           w_down[l], b_down[l])
        states.append(st)
    y_prompt = x_all[:B * S].reshape(B, S, D)
    y_sample = x_all[B * S:].reshape(DB, DS, D)
    k_prompt = jnp.stack([s[0] for s in states])
    v_prompt = jnp.stack([s[1] for s in states])
    mem_k_prompt = jnp.stack([s[2] for s in states])
    mem_v_prompt = jnp.stack([s[3] for s in states])
    k_sample = jnp.stack([s[4] for s in states])
    v_sample = jnp.stack([s[5] for s in states])
    gmlp_v_sample = jnp.stack([s[6] for s in states])
    return (y_prompt, y_sample, k_prompt, v_prompt, mem_k_prompt, mem_v_prompt, k_sample, v_sample, gmlp_v_sample)
```

```python
import functools
import math

import numpy as np
import jax
import jax.numpy as jnp
from jax import lax
from jax.experimental import pallas as pl
from jax.experimental.pallas import tpu as pltpu

MOBA_BLOCK = 256
MOBA_TOPK = 3
REL_BUCKETS = 32
REL_MAX_DISTANCE = 128
TOP_K = 4
SWIGLU_LIMIT = 7.0
SWIGLU_ALPHA = 1.702
NORM_EPS = 1e-6

V7X_VMEM_BYTES = 64 * 1024 * 1024
VMEM_LIMIT = V7X_VMEM_BYTES - 12 * 1024 * 1024
LANES = 128

ROW_TILE = 1040
COL_TILE = 256
NORM_ROWS = 256
MOE_CHUNK = 256
MOE_VISIT_ROWS = 1280
COMBINE_TOKENS = 32

F32 = jnp.float32
BF16 = jnp.bfloat16
HI = lax.Precision.HIGHEST
NT = (((1,), (1,)), ((), ()))
NEG_INF = float("-inf")


def _params(sem, vmem=VMEM_LIMIT):
    return pltpu.CompilerParams(dimension_semantics=sem, vmem_limit_bytes=vmem)


def _tile(n, pref):
    return n if n <= pref else pref


def _rmsnorm_kernel(x_ref, g_ref, o_ref):
    x = x_ref[...]
    y = x * lax.rsqrt(jnp.mean(x * x, axis=-1, keepdims=True) + NORM_EPS)
    o_ref[...] = (y * g_ref[...]).astype(o_ref.dtype)


def _rmsnorm(x, g, out_dtype):
    n, d = x.shape
    tm = _tile(n, NORM_ROWS)
    return pl.pallas_call(
        _rmsnorm_kernel,
        grid=(pl.cdiv(n, tm),),
        in_specs=[pl.BlockSpec((tm, d), lambda i: (i, 0)), pl.BlockSpec((1, d), lambda i: (0, 0))],
        out_specs=pl.BlockSpec((tm, d), lambda i: (i, 0)),
        out_shape=jax.ShapeDtypeStruct((n, d), out_dtype),
        compiler_params=_params(("parallel",)),
        name="rmsnorm",
    )(x, g.reshape(1, d))


def _proj_kernel(*refs, epilogue, head_dim):
    x_ref, w_ref, o_ref = refs[0], refs[1], refs[-1]
    acc = jnp.dot(x_ref[...], w_ref[...], preferred_element_type=F32)
    if epilogue == "headnorm":
        g = refs[2][...]
        for s in range(acc.shape[1] // head_dim):
            a = acc[:, s * head_dim:(s + 1) * head_dim]
            y = a * lax.rsqrt(jnp.mean(a * a, axis=-1, keepdims=True) + NORM_EPS)
            o_ref[:, s * head_dim:(s + 1) * head_dim] = (y * g).astype(o_ref.dtype)
    elif epilogue == "gelu":
        o_ref[...] = (acc * (lax.erf(acc * (1.0 / math.sqrt(2.0))) + 1.0) * 0.5).astype(o_ref.dtype)
    elif epilogue == "sigmoid":
        o_ref[...] = jax.nn.sigmoid(acc).astype(o_ref.dtype)
    elif epilogue == "residual":
        o_ref[...] = (refs[2][...] + acc).astype(o_ref.dtype)
    else:
        o_ref[...] = acc.astype(o_ref.dtype)


def _proj(x, w, col_off, width, epilogue="none", extra=None, head_dim=0, out_dtype=F32):
    n, k = x.shape
    tm = _tile(n, ROW_TILE)
    tn = _tile(width, COL_TILE)
    assert width % tn == 0 and col_off % tn == 0
    joff = col_off // tn
    in_specs = [pl.BlockSpec((tm, k), lambda i, j: (i, 0)), pl.BlockSpec((k, tn), lambda i, j: (0, joff + j))]
    args = [x, w]
    if epilogue == "headnorm":
        assert tn % head_dim == 0
        in_specs.append(pl.BlockSpec((1, head_dim), lambda i, j: (0, 0)))
        args.append(extra.reshape(1, head_dim))
    elif epilogue == "residual":
        in_specs.append(pl.BlockSpec((tm, tn), lambda i, j: (i, j)))
        args.append(extra)
    return pl.pallas_call(
        functools.partial(_proj_kernel, epilogue=epilogue, head_dim=head_dim),
        grid=(pl.cdiv(n, tm), width // tn),
        in_specs=in_specs,
        out_specs=pl.BlockSpec((tm, tn), lambda i, j: (i, j)),
        out_shape=jax.ShapeDtypeStruct((n, width), out_dtype),
        compiler_params=_params(("parallel", "arbitrary")),
        name="proj_" + epilogue,
    )(*args)


def _t5_bucket_np(dist):
    n = np.maximum(dist, 0)
    max_exact = REL_BUCKETS // 2
    log_ratio = (np.log(np.maximum(n, 1).astype(np.float32) / np.float32(max_exact))
                 / np.float32(math.log(REL_MAX_DISTANCE / max_exact)))
    large = max_exact + (log_ratio * np.float32(REL_BUCKETS - max_exact)).astype(np.int32)
    return np.where(n < max_exact, n, np.minimum(large, REL_BUCKETS - 1)).astype(np.int32)


def _bias_table_kernel(rel_ref, bucket_ref, o_ref):
    h = pl.program_id(0)
    for t in range(2):
        bucket = bucket_ref[t]
        acc = jnp.zeros(bucket.shape, F32)
        for b in range(REL_BUCKETS):
            acc = jnp.where(bucket == b, rel_ref[b, h], acc)
        o_ref[0, t] = acc


def _bias_tables(rel_bias):
    heads = rel_bias.shape[1]
    i = np.arange(MOBA_BLOCK)[:, None]
    j = np.arange(MOBA_BLOCK)[None, :]
    buckets = jnp.asarray(np.stack([_t5_bucket_np(i - j), _t5_bucket_np(MOBA_BLOCK + i - j)]))
    return pl.pallas_call(
        _bias_table_kernel,
        grid=(heads,),
        in_specs=[pl.BlockSpec(memory_space=pltpu.SMEM),
                  pl.BlockSpec((2, MOBA_BLOCK, MOBA_BLOCK), lambda h: (0, 0, 0))],
        out_specs=pl.BlockSpec((1, 2, MOBA_BLOCK, MOBA_BLOCK), lambda h: (h, 0, 0, 0)),
        out_shape=jax.ShapeDtypeStruct((heads, 2, MOBA_BLOCK, MOBA_BLOCK), F32),
        compiler_params=_params(("arbitrary",)),
        name="bias_tables",
    )(rel_bias, buckets)


def _top_blocks(gate, lane_iota, n_lanes):
    picked = []
    mask = jnp.zeros(gate.shape, jnp.bool_)
    g = gate
    for _ in range(MOBA_TOPK):
        m = jnp.max(g, axis=1, keepdims=True)
        first = jnp.min(jnp.where((g == m) & (m > NEG_INF), lane_iota, n_lanes), axis=1, keepdims=True)
        pick = lane_iota == first
        picked.append(first)
        mask = mask | pick
        g = jnp.where(pick, NEG_INF, g)
    return picked, mask


def _moba_prompt_kernel(q_ref, k_ref, v_ref, bias_ref, o_ref, kmean_ref, *, scale):
    qb = pl.program_id(1)
    blk = MOBA_BLOCK
    s_len, hd = k_ref.shape
    nb = s_len // blk

    @pl.when(qb == 0)
    def _():
        kmean_ref[...] = jnp.mean(k_ref[...].reshape(nb, blk, hd), axis=1)

    q = q_ref[...]
    gate = lax.dot_general(q, kmean_ref[...], NT, precision=HI, preferred_element_type=F32)
    blk_iota = lax.broadcasted_iota(jnp.int32, (blk, nb), 1)
    gate = jnp.where(blk_iota < qb, gate, NEG_INF)
    _, sel = _top_blocks(gate, blk_iota, nb)
    sel_f = sel.astype(F32)

    qbf = q.astype(BF16)
    row = lax.broadcasted_iota(jnp.int32, (blk, blk), 0)
    col = lax.broadcasted_iota(jnp.int32, (blk, blk), 1)

    def logits(start):
        kb = k_ref[pl.ds(start, blk), :].astype(BF16)
        return lax.dot_general(qbf, kb, NT, preferred_element_type=F32) * scale

    def pv(p, start):
        return jnp.dot(p.astype(BF16), v_ref[pl.ds(start, blk), :].astype(BF16), preferred_element_type=F32)

    own = pl.multiple_of(qb * blk, blk)
    s = jnp.where(row >= col, logits(own) + bias_ref[0, 0], NEG_INF)
    m = jnp.max(s, axis=1, keepdims=True)
    p = jnp.exp(s - m)
    l = jnp.sum(p, axis=1, keepdims=True)
    acc = pv(p, own)

    far_bias = bias_ref[0, 1, blk - 1:blk, 0:1]

    def body(kb, carry):
        m, l, acc = carry
        start = pl.multiple_of(kb * blk, blk)
        bias = jnp.where(kb == qb - 1, bias_ref[0, 1], far_bias)
        chosen = jnp.max(jnp.where(blk_iota == kb, sel_f, 0.0), axis=1, keepdims=True) > 0.0
        s = jnp.where(chosen, logits(start) + bias, NEG_INF)
        m_new = jnp.maximum(m, jnp.max(s, axis=1, keepdims=True))
        alpha = jnp.exp(m - m_new)
        p = jnp.exp(s - m_new)
        return m_new, alpha * l + jnp.sum(p, axis=1, keepdims=True), alpha * acc + pv(p, start)

    m, l, acc = lax.fori_loop(0, qb, body, (m, l, acc))
    o_ref[...] = (acc / l).astype(o_ref.dtype)


def _moba_prompt(q, k, v, bias_tab, s_len, heads, hd):
    nqb = s_len // MOBA_BLOCK
    return pl.pallas_call(
        functools.partial(_moba_prompt_kernel, scale=hd ** -0.5),
        grid=(heads, nqb),
        in_specs=[pl.BlockSpec((MOBA_BLOCK, hd), lambda h, i: (i, h)),
                  pl.BlockSpec((s_len, hd), lambda h, i: (0, h)),
                  pl.BlockSpec((s_len, hd), lambda h, i: (0, h)),
                  pl.BlockSpec((1, 2, MOBA_BLOCK, MOBA_BLOCK), lambda h, i: (h, 0, 0, 0))],
        out_specs=pl.BlockSpec((MOBA_BLOCK, hd), lambda h, i: (i, h)),
        out_shape=jax.ShapeDtypeStruct((s_len, heads * hd), BF16),
        scratch_shapes=[pltpu.VMEM((s_len // MOBA_BLOCK, hd), F32)],
        compiler_params=_params(("parallel", "arbitrary")),
        name="moba_prompt",
    )(q, k, v, bias_tab)


def _sample_gate_kernel(pt_ref, q_ref, *refs, ppb, heads, hd):
    page_refs, sel_ref, kmean_ref = refs[:ppb], refs[ppb], refs[ppb + 1]
    j = pl.program_id(1)
    nb = kmean_ref.shape[0]
    total = jnp.sum(page_refs[0][0], axis=0, keepdims=True)
    for p in range(1, ppb):
        total = total + jnp.sum(page_refs[p][0], axis=0, keepdims=True)
    kmean_ref[pl.ds(j, 1), :] = total * (1.0 / MOBA_BLOCK)

    @pl.when(j == nb - 1)
    def _():
        width = heads * hd
        prod = kmean_ref[...] * q_ref[0]
        seg = (lax.broadcasted_iota(jnp.int32, (heads, width), 1) // hd
               == lax.broadcasted_iota(jnp.int32, (heads, width), 0)).astype(F32)
        gate = lax.dot_general(seg, prod, NT, precision=HI, preferred_element_type=F32)
        blk_iota = lax.broadcasted_iota(jnp.int32, (heads, nb), 1)
        picked, _ = _top_blocks(gate, blk_iota, nb)
        lane = lax.broadcasted_iota(jnp.int32, (heads, LANES), 1)
        out = jnp.zeros((heads, LANES), jnp.int32)
        for t in range(MOBA_TOPK):
            out = jnp.where(lane == t, picked[t], out)
        sel_ref[0] = out


def _sample_gate(page_table_flat, q_s3, cache_k3, n_seq, nb, ppb, heads, hd):
    n_pages = nb * ppb
    page, width = cache_k3.shape[1:]
    page_specs = [pl.BlockSpec((1, page, width),
                               lambda b, j, pt, p=p: (pt[b * n_pages + j * ppb + p], 0, 0)) for p in range(ppb)]
    return pl.pallas_call(
        functools.partial(_sample_gate_kernel, ppb=ppb, heads=heads, hd=hd),
        grid_spec=pltpu.PrefetchScalarGridSpec(
            num_scalar_prefetch=1,
            grid=(n_seq, nb),
            in_specs=[pl.BlockSpec((1, 1, width), lambda b, j, pt: (b, 0, 0))] + page_specs,
            out_specs=pl.BlockSpec((1, heads, LANES), lambda b, j, pt: (b, 0, 0)),
            scratch_shapes=[pltpu.VMEM((nb, width), F32)]),
        out_shape=jax.ShapeDtypeStruct((n_seq, heads, LANES), jnp.int32),
        compiler_params=_params(("parallel", "arbitrary")),
        name="moba_sample_gate",
    )(page_table_flat, q_s3, *([cache_k3] * ppb))


def _moba_sample_kernel(pt_ref, sel_ref, q_ref, kn_ref, vn_ref, sb_ref, *refs, n_slots, nb, scale):
    k_refs, v_refs, o_ref = refs[:n_slots], refs[n_slots:2 * n_slots], refs[-1]
    b, h = pl.program_id(0), pl.program_id(1)
    heads = pl.num_programs(1)
    ppb = n_slots // MOBA_TOPK
    q = q_ref[0]
    q8 = jnp.broadcast_to(q, (8, q.shape[1])).astype(BF16)
    keys = jnp.concatenate([r[0] for r in k_refs], axis=0).astype(BF16)
    vals = jnp.concatenate([r[0] for r in v_refs], axis=0).astype(BF16)
    near, far, self_bias = sb_ref[0, 0:1, :], sb_ref[0, 1:2, :], sb_ref[0, 2:3, 0:1]
    bias = jnp.concatenate(
        [jnp.where(sel_ref[(b * heads + h) * MOBA_TOPK + t] == nb - 1, near, far) for t in range(MOBA_TOPK)], axis=1)
    s = lax.dot_general(q8, keys, NT, preferred_element_type=F32)[0:1] * scale + bias
    s_self = jnp.sum(q * kn_ref[0], axis=1, keepdims=True) * scale + self_bias
    m = jnp.maximum(jnp.max(s, axis=1, keepdims=True), s_self)
    p = jnp.exp(s - m)
    p_self = jnp.exp(s_self - m)
    l = jnp.sum(p, axis=1, keepdims=True) + p_self
    p8 = jnp.broadcast_to(p, (8, p.shape[1])).astype(BF16)
    acc = jnp.dot(p8, vals, preferred_element_type=F32)[0:1] + p_self * vn_ref[0]
    o_ref[0] = (acc / l).astype(o_ref.dtype)
    del ppb


def _moba_sample(page_table_flat, sel_flat, q_s3, k_s3, v_s3, sample_bias, cache_k3, cache_v3, n_seq, nb, ppb, heads, hd):
    n_pages = nb * ppb
    page = cache_k3.shape[1]
    n_slots = MOBA_TOPK * ppb

    def page_spec(t, p):
        def index(b, h, pt, sel):
            return (pt[b * n_pages + sel[(b * heads + h) * MOBA_TOPK + t] * ppb + p], 0, h)
        return pl.BlockSpec((1, page, hd), index)

    kv_specs = [page_spec(t, p) for t in range(MOBA_TOPK) for p in range(ppb)]
    row_spec = pl.BlockSpec((1, 1, hd), lambda b, h, pt, sel: (b, 0, h))
    return pl.pallas_call(
        functools.partial(_moba_sample_kernel, n_slots=n_slots, nb=nb, scale=hd ** -0.5),
        grid_spec=pltpu.PrefetchScalarGridSpec(
            num_scalar_prefetch=2,
            grid=(n_seq, heads),
            in_specs=[row_spec, row_spec, row_spec,
                      pl.BlockSpec((1, 3, MOBA_BLOCK), lambda b, h, pt, sel: (h, 0, 0))] + kv_specs + kv_specs,
            out_specs=row_spec),
        out_shape=jax.ShapeDtypeStruct((n_seq, 1, heads * hd), BF16),
        compiler_params=_params(("parallel", "arbitrary")),
        name="moba_sample",
    )(page_table_flat, sel_flat, q_s3, k_s3, v_s3, sample_bias, *([cache_k3] * n_slots), *([cache_v3] * n_slots))


def _layernorm(x, g, b):
    mu = jnp.mean(x, axis=-1, keepdims=True)
    var = jnp.mean(jnp.square(x - mu), axis=-1, keepdims=True)
    return (x - mu) * lax.rsqrt(var + NORM_EPS) * g + b


def _gmlp_prompt_kernel(z_ref, g_ref, b_ref, ws_ref, bs_ref, o_ref):
    groups, chunk, _ = ws_ref.shape
    gw = g_ref.shape[1]
    gd = gw // groups
    u = z_ref[:, :gw]
    gv = _layernorm(z_ref[:, gw:], g_ref[...], b_ref[...]).astype(BF16)
    row = lax.broadcasted_iota(jnp.int32, (chunk, chunk), 0)
    col = lax.broadcasted_iota(jnp.int32, (chunk, chunk), 1)
    for g in range(groups):
        ws = jnp.where(row >= col, ws_ref[g], 0.0).astype(BF16)
        mixed = jnp.dot(ws, gv[:, g * gd:(g + 1) * gd], preferred_element_type=F32) + bs_ref[:, g:g + 1]
        o_ref[:, g * gd:(g + 1) * gd] = (u[:, g * gd:(g + 1) * gd] * mixed).astype(o_ref.dtype)


def _gmlp_sample_kernel(z_ref, g_ref, b_ref, w0_ref, b0_ref, o_ref, gv_ref):
    gw = g_ref.shape[1]
    gv = _layernorm(z_ref[:, gw:], g_ref[...], b_ref[...])
    gv_ref[...] = gv
    o_ref[...] = (z_ref[:, :gw] * (w0_ref[...] * gv + b0_ref[...])).astype(o_ref.dtype)


def _gmlp(z, s_len, ln_g, ln_b, ws, bs):
    n, two_gw = z.shape
    gw = two_gw // 2
    groups, chunk, _ = ws.shape
    gd = gw // groups
    assert s_len % chunk == 0
    g2, b2 = ln_g.reshape(1, gw), ln_b.reshape(1, gw)
    out_p = pl.pallas_call(
        _gmlp_prompt_kernel,
        grid=(s_len // chunk,),
        in_specs=[pl.BlockSpec((chunk, two_gw), lambda c: (c, 0)),
                  pl.BlockSpec((1, gw), lambda c: (0, 0)), pl.BlockSpec((1, gw), lambda c: (0, 0)),
                  pl.BlockSpec((groups, chunk, chunk), lambda c: (0, 0, 0)),
                  pl.BlockSpec((chunk, groups), lambda c: (0, 0))],
        out_specs=pl.BlockSpec((chunk, gw), lambda c: (c, 0)),
        out_shape=jax.ShapeDtypeStruct((s_len, gw), BF16),
        compiler_params=_params(("parallel",)),
        name="gmlp_prompt",
    )(z, g2, b2, ws, bs.T)
    z_s = z[s_len:]
    n_s = n - s_len
    w0 = jnp.repeat(ws[:, 0, 0], gd).reshape(1, gw)
    b0 = jnp.repeat(bs[:, 0], gd).reshape(1, gw)
    out_s, gv_s = pl.pallas_call(
        _gmlp_sample_kernel,
        out_shape=(jax.ShapeDtypeStruct((n_s, gw), BF16), jax.ShapeDtypeStruct((n_s, gw), F32)),
        name="gmlp_sample",
    )(z_s, g2, b2, w0, b0)
    return out_p, out_s, gv_s


def _mem_attn_kernel(q_ref, k_ref, v_ref, o_ref, *, heads, scale, batched):
    q = q_ref[0] if batched else q_ref[...]
    k = k_ref[0] if batched else k_ref[...]
    v = v_ref[0] if batched else v_ref[...]
    rows = max(q.shape[0], 8)
    hd = q.shape[1] // heads
    for h in range(heads):
        sl = slice(h * hd, (h + 1) * hd)
        qh = jnp.broadcast_to(q[:, sl], (rows, hd)).astype(BF16)
        s = lax.dot_general(qh, k[:, sl].astype(BF16), NT, preferred_element_type=F32) * scale
        p = jnp.exp(s - jnp.max(s, axis=1, keepdims=True))
        l = jnp.sum(p, axis=1, keepdims=True)
        out = jnp.dot(p.astype(BF16), v[:, sl].astype(BF16), preferred_element_type=F32) / l
        out = out[:q.shape[0]].astype(o_ref.dtype)
        if batched:
            o_ref[0, :, sl] = out
        else:
            o_ref[:, sl] = out


def _mem_attn_prompt(mq, mk, mv, s_len, heads):
    width = mq.shape[1]
    mem = mk.shape[0]
    tq = _tile(s_len, 512)
    return pl.pallas_call(
        functools.partial(_mem_attn_kernel, heads=heads, scale=(width // heads) ** -0.5, batched=False),
        grid=(s_len // tq,),
        in_specs=[pl.BlockSpec((tq, width), lambda i: (i, 0)),
                  pl.BlockSpec((mem, width), lambda i: (0, 0)), pl.BlockSpec((mem, width), lambda i: (0, 0))],
        out_specs=pl.BlockSpec((tq, width), lambda i: (i, 0)),
        out_shape=jax.ShapeDtypeStruct((s_len, width), BF16),
        compiler_params=_params(("parallel",)),
        name="mem_attn_prompt",
    )(mq, mk, mv)


def _mem_attn_sample(mq_s3, ck, cv, heads):
    n_seq, mem, width = ck.shape
    return pl.pallas_call(
        functools.partial(_mem_attn_kernel, heads=heads, scale=(width // heads) ** -0.5, batched=True),
        grid=(n_seq,),
        in_specs=[pl.BlockSpec((1, 1, width), lambda b: (b, 0, 0)),
                  pl.BlockSpec((1, mem, width), lambda b: (b, 0, 0)), pl.BlockSpec((1, mem, width), lambda b: (b, 0, 0))],
        out_specs=pl.BlockSpec((1, 1, width), lambda b: (b, 0, 0)),
        out_shape=jax.ShapeDtypeStruct((n_seq, 1, width), BF16),
        compiler_params=_params(("parallel",)),
        name="mem_attn_sample",
    )(mq_s3, ck, cv)


def _merge_kernel(a_ref, g_ref, m_ref, ga_ref, gg_ref, gm_ref, wa_ref, wg_ref, wm_ref, o_ref):
    mixed = ga_ref[...] * jnp.dot(a_ref[...], wa_ref[...], preferred_element_type=F32)
    mixed = mixed + gg_ref[...] * jnp.dot(g_ref[...], wg_ref[...], preferred_element_type=F32)
    mixed = mixed + gm_ref[...] * jnp.dot(m_ref[...], wm_ref[...], preferred_element_type=F32)
    o_ref[...] = mixed.astype(o_ref.dtype)


def _merge(att, gout, mout, gates, w_a, w_g, w_m):
    n = att.shape[0]
    d = w_a.shape[1]
    tm = _tile(n, ROW_TILE)
    tn = _tile(d, COL_TILE)
    nj = d // tn

    def act_spec(a):
        return pl.BlockSpec((tm, a.shape[1]), lambda i, j: (i, 0))

    def gate_spec(branch):
        return pl.BlockSpec((tm, tn), lambda i, j: (i, branch * nj + j))

    def w_spec(w):
        return pl.BlockSpec((w.shape[0], tn), lambda i, j: (0, j))

    return pl.pallas_call(
        _merge_kernel,
        grid=(pl.cdiv(n, tm), nj),
        in_specs=[act_spec(att), act_spec(gout), act_spec(mout), gate_spec(0), gate_spec(1), gate_spec(2),
                  w_spec(w_a), w_spec(w_g), w_spec(w_m)],
        out_specs=pl.BlockSpec((tm, tn), lambda i, j: (i, j)),
        out_shape=jax.ShapeDtypeStruct((n, d), BF16),
        compiler_params=_params(("parallel", "arbitrary")),
        name="merge",
    )(att, gout, mout, gates, gates, gates, w_a, w_g, w_m)


def _router_kernel(h_ref, g_ref, wr_ref, br_ref, hn_ref, mi_ref, mw_ref, cnt_ref, carry_ref, *, n_tokens, n_experts):
    i = pl.program_id(0)
    tm = h_ref.shape[0]

    @pl.when(i == 0)
    def _():
        carry_ref[...] = jnp.zeros(carry_ref.shape, F32)

    x = h_ref[...]
    hn = x * lax.rsqrt(jnp.mean(x * x, axis=-1, keepdims=True) + NORM_EPS) * g_ref[...]
    hn_ref[...] = hn
    logits = jnp.dot(hn, wr_ref[...], precision=HI, preferred_element_type=F32) + br_ref[...]
    e_iota = lax.broadcasted_iota(jnp.int32, (tm, n_experts), 1)
    vals, ids = [], []
    l = logits
    for _ in range(TOP_K):
        m = jnp.max(l, axis=1, keepdims=True)
        first = jnp.min(jnp.where(l == m, e_iota, n_experts), axis=1, keepdims=True)
        vals.append(m)
        ids.append(first)
        l = jnp.where(e_iota == first, NEG_INF, l)
    ex = [jnp.exp(v - vals[0]) for v in vals]
    denom = ex[0]
    for e in ex[1:]:
        denom = denom + e
    valid = (i * tm + lax.broadcasted_iota(jnp.int32, (tm, 1), 0)) < n_tokens
    multi_hot = jnp.zeros((tm, n_experts), F32)
    for k in range(TOP_K):
        multi_hot = multi_hot + (e_iota == ids[k]).astype(F32)
    multi_hot = jnp.where(valid, multi_hot, 0.0)
    lower = (lax.broadcasted_iota(jnp.int32, (tm, tm), 0) > lax.broadcasted_iota(jnp.int32, (tm, tm), 1)).astype(BF16)
    before = jnp.dot(lower, multi_hot.astype(BF16), preferred_element_type=F32) + carry_ref[:, :n_experts]
    lane = lax.broadcasted_iota(jnp.int32, (tm, LANES), 1)
    meta_i = jnp.zeros((tm, LANES), jnp.int32)
    meta_w = jnp.zeros((tm, LANES), F32)
    for k in range(TOP_K):
        rank = jnp.sum(jnp.where(e_iota == ids[k], before, 0.0), axis=1, keepdims=True).astype(jnp.int32)
        meta_i = jnp.where(lane == k, ids[k], meta_i)
        meta_i = jnp.where(lane == TOP_K + k, rank, meta_i)
        meta_w = jnp.where(lane == k, ex[k] / denom, meta_w)
    mi_ref[...] = meta_i
    mw_ref[...] = meta_w
    carry_ref[:, :n_experts] = carry_ref[:, :n_experts] + jnp.sum(multi_hot, axis=0, keepdims=True)
    cnt_ref[...] = carry_ref[...].astype(jnp.int32)


def _router(h, g, w_router, b_router):
    n, d = h.shape
    n_experts = w_router.shape[1]
    assert n_experts <= LANES
    tm = _tile(n, NORM_ROWS)
    return pl.pallas_call(
        functools.partial(_router_kernel, n_tokens=n, n_experts=n_experts),
        grid=(pl.cdiv(n, tm),),
        in_specs=[pl.BlockSpec((tm, d), lambda i: (i, 0)), pl.BlockSpec((1, d), lambda i: (0, 0)),
                  pl.BlockSpec((d, n_experts), lambda i: (0, 0)), pl.BlockSpec((1, n_experts), lambda i: (0, 0))],
        out_specs=[pl.BlockSpec((tm, d), lambda i: (i, 0)), pl.BlockSpec((tm, LANES), lambda i: (i, 0)),
                   pl.BlockSpec((tm, LANES), lambda i: (i, 0)), pl.BlockSpec((1, LANES), lambda i: (0, 0))],
        out_shape=(jax.ShapeDtypeStruct((n, d), F32), jax.ShapeDtypeStruct((n, LANES), jnp.int32),
                   jax.ShapeDtypeStruct((n, LANES), F32), jax.ShapeDtypeStruct((1, LANES), jnp.int32)),
        scratch_shapes=[pltpu.VMEM((1, LANES), F32)],
        compiler_params=_params(("arbitrary",)),
        name="router",
    )(h, g.reshape(1, d), w_router, b_router.reshape(1, n_experts))


def _dispatch_kernel(valid_ref, src_ref, hn_ref, o_ref, buf_ref, sem):
    c = pl.program_id(0)
    rows = buf_ref.shape[0]

    def row_copy(r, src_row):
        return pltpu.make_async_copy(hn_ref.at[pl.ds(src_row, 1)], buf_ref.at[pl.ds(r, 1)], sem)

    @pl.when(valid_ref[c] > 0)
    def _():
        def start(r, carry):
            row_copy(r, src_ref[r]).start()
            return carry

        def wait(r, carry):
            row_copy(r, 0).wait()
            return carry

        lax.fori_loop(0, rows, start, 0)
        lax.fori_loop(0, rows, wait, 0)
        o_ref[...] = buf_ref[...].astype(o_ref.dtype)

    @pl.when(valid_ref[c] == 0)
    def _():
        o_ref[...] = jnp.zeros(o_ref.shape, o_ref.dtype)


def _dispatch(chunk_valid, src_rows, hn, n_rows):
    d = hn.shape[1]
    return pl.pallas_call(
        _dispatch_kernel,
        grid_spec=pltpu.PrefetchScalarGridSpec(
            num_scalar_prefetch=1,
            grid=(n_rows // MOE_CHUNK,),
            in_specs=[pl.BlockSpec((MOE_CHUNK,), lambda c, valid: (c,), memory_space=pltpu.SMEM),
                      pl.BlockSpec(memory_space=pl.ANY)],
            out_specs=pl.BlockSpec((MOE_CHUNK, d), lambda c, valid: (c, 0)),
            scratch_shapes=[pltpu.VMEM((MOE_CHUNK, d), F32), pltpu.SemaphoreType.DMA(())]),
        out_shape=jax.ShapeDtypeStruct((n_rows, d), BF16),
        compiler_params=_params(("arbitrary",)),
        name="moe_dispatch",
    )(chunk_valid, src_rows, hn)


def _visit_specs(n_col_tiles):
    def col(v, j, vr):
        return jnp.where(vr[v] > 0, j, n_col_tiles - 1)

    def rows(v, j, ve, vb, vr):
        return (vb[v], 0)

    def weight(v, j, ve, vb, vr):
        return (ve[v], 0, col(v, j, vr))

    def out(v, j, ve, vb, vr):
        return (v, j)

    return rows, weight, out


def _zero_tail_chunks(o_ref, n_chunks):
    def body(c, carry):
        o_ref[pl.ds(pl.multiple_of(c * MOE_CHUNK, MOE_CHUNK), MOE_CHUNK), :] = jnp.zeros((MOE_CHUNK, o_ref.shape[1]), o_ref.dtype)
        return carry

    lax.fori_loop(n_chunks, o_ref.shape[0] // MOE_CHUNK, body, 0)


def _gate_up_kernel(ve_ref, vb_ref, vr_ref, x_ref, w_ref, b_ref, p_ref, o_ref, wb_ref):
    v = pl.program_id(0)
    wb_ref[...] = w_ref[0].astype(BF16)
    tn = wb_ref.shape[1]
    even = (lax.broadcasted_iota(jnp.int32, (MOE_CHUNK, tn), 1) % 2) == 0

    def body(c, carry):
        r0 = pl.multiple_of(c * MOE_CHUNK, MOE_CHUNK)
        gu = jnp.dot(x_ref[pl.ds(r0, MOE_CHUNK), :], wb_ref[...], preferred_element_type=F32) + b_ref[0]
        nxt = pltpu.roll(gu, tn - 1, 1)
        glu = jnp.minimum(gu, SWIGLU_LIMIT)
        lin = jnp.clip(nxt, -SWIGLU_LIMIT, SWIGLU_LIMIT)
        act = jnp.where(even, glu * jax.nn.sigmoid(SWIGLU_ALPHA * glu) * (lin + 1.0), 0.0)
        o_ref[pl.ds(r0, MOE_CHUNK), :] = jnp.dot(act.astype(BF16), p_ref[...], preferred_element_type=F32).astype(o_ref.dtype)
        return carry

    n_chunks = (vr_ref[v] + MOE_CHUNK - 1) // MOE_CHUNK
    lax.fori_loop(0, n_chunks, body, 0)
    _zero_tail_chunks(o_ref, n_chunks)


def _gate_up(vis_expert, vis_block, vis_rows, xs, w_gate_up, b_gate_up):
    n_rows, d = xs.shape
    n_experts, _, two_f = w_gate_up.shape
    tn = _tile(two_f, COL_TILE)
    sel = np.zeros((tn, tn // 2), np.float32)
    sel[2 * np.arange(tn // 2), np.arange(tn // 2)] = 1.0
    rows, weight, out = _visit_specs(two_f // tn)
    return pl.pallas_call(
        _gate_up_kernel,
        grid_spec=pltpu.PrefetchScalarGridSpec(
            num_scalar_prefetch=3,
            grid=(n_rows // MOE_VISIT_ROWS, two_f // tn),
            in_specs=[pl.BlockSpec((MOE_VISIT_ROWS, d), rows),
                      pl.BlockSpec((1, d, tn), weight),
                      pl.BlockSpec((1, 1, tn), weight),
                      pl.BlockSpec((tn, tn // 2), lambda v, j, ve, vb, vr: (0, 0))],
            out_specs=pl.BlockSpec((MOE_VISIT_ROWS, tn // 2), out),
            scratch_shapes=[pltpu.VMEM((d, tn), BF16)]),
        out_shape=jax.ShapeDtypeStruct((n_rows, two_f // 2), BF16),
        compiler_params=_params(("arbitrary", "arbitrary")),
        name="moe_gate_up",
    )(vis_expert, vis_block, vis_rows, xs, w_gate_up, b_gate_up.reshape(n_experts, 1, two_f), jnp.asarray(sel, BF16))


def _down_kernel(ve_ref, vb_ref, vr_ref, x_ref, w_ref, b_ref, rw_ref, o_ref, wb_ref):
    v = pl.program_id(0)
    wb_ref[...] = w_ref[0].astype(BF16)

    def body(c, carry):
        r0 = pl.multiple_of(c * MOE_CHUNK, MOE_CHUNK)
        y = jnp.dot(x_ref[pl.ds(r0, MOE_CHUNK), :], wb_ref[...], preferred_element_type=F32) + b_ref[0]
        o_ref[pl.ds(r0, MOE_CHUNK), :] = rw_ref[pl.ds(r0, MOE_CHUNK), :] * y
        return carry

    n_chunks = (vr_ref[v] + MOE_CHUNK - 1) // MOE_CHUNK
    lax.fori_loop(0, n_chunks, body, 0)
    _zero_tail_chunks(o_ref, n_chunks)


def _down(vis_expert, vis_block, vis_rows, act, w_down, b_down, row_w):
    n_rows, f = act.shape
    n_experts, _, d = w_down.shape
    tn = _tile(d, COL_TILE)
    rows, weight, out = _visit_specs(d // tn)
    return pl.pallas_call(
        _down_kernel,
        grid_spec=pltpu.PrefetchScalarGridSpec(
            num_scalar_prefetch=3,
            grid=(n_rows // MOE_VISIT_ROWS, d // tn),
            in_specs=[pl.BlockSpec((MOE_VISIT_ROWS, f), rows),
                      pl.BlockSpec((1, f, tn), weight),
                      pl.BlockSpec((1, 1, tn), weight),
                      pl.BlockSpec((MOE_VISIT_ROWS, 1), rows)],
            out_specs=pl.BlockSpec((MOE_VISIT_ROWS, tn), out),
            scratch_shapes=[pltpu.VMEM((f, tn), BF16)]),
        out_shape=jax.ShapeDtypeStruct((n_rows, d), F32),
        compiler_params=_params(("arbitrary", "arbitrary")),
        name="moe_down",
    )(vis_expert, vis_block, vis_rows, act, w_down, b_down.reshape(n_experts, 1, d), row_w)


def _combine_kernel(pos_ref, h_ref, y_ref, o_ref, buf_ref, sem):
    tt = h_ref.shape[0]

    def row_copy(t, k, src_row):
        return pltpu.make_async_copy(y_ref.at[pl.ds(src_row, 1)], buf_ref.at[pl.ds(k * tt + t, 1)], sem)

    def start(t, carry):
        for k in range(TOP_K):
            row_copy(t, k, pos_ref[t * TOP_K + k]).start()
        return carry

    def wait(t, carry):
        for k in range(TOP_K):
            row_copy(t, k, 0).wait()
        return carry

    lax.fori_loop(0, tt, start, 0)
    lax.fori_loop(0, tt, wait, 0)
    out = h_ref[...]
    for k in range(TOP_K):
        out = out + buf_ref[k * tt:(k + 1) * tt, :]
    o_ref[...] = out


def _combine(pos_flat, h, y_sorted):
    n, d = h.shape
    tt = COMBINE_TOKENS
    n_tiles = pl.cdiv(n, tt)
    pos_flat = jnp.pad(pos_flat, (0, n_tiles * tt * TOP_K - pos_flat.shape[0]))
    return pl.pallas_call(
        _combine_kernel,
        grid=(n_tiles,),
        in_specs=[pl.BlockSpec((tt * TOP_K,), lambda i: (i,), memory_space=pltpu.SMEM),
                  pl.BlockSpec((tt, d), lambda i: (i, 0)),
                  pl.BlockSpec(memory_space=pl.ANY)],
        out_specs=pl.BlockSpec((tt, d), lambda i: (i, 0)),
        out_shape=jax.ShapeDtypeStruct((n, d), F32),
        scratch_shapes=[pltpu.VMEM((TOP_K * tt, d), F32), pltpu.SemaphoreType.DMA(())],
        compiler_params=_params(("arbitrary",)),
        name="moe_combine",
    )(pos_flat, h, y_sorted)


def _moe(h, norm2_g, w_router, b_router, w_gate_up, b_gate_up, w_down, b_down):
    n, d = h.shape
    n_experts = w_router.shape[1]
    hn, meta_i, meta_w, counts = _router(h, norm2_g, w_router, b_router)
    ids, rank, weights = meta_i[:, :TOP_K], meta_i[:, TOP_K:2 * TOP_K], meta_w[:, :TOP_K]
    counts = counts[0, :n_experts]

    max_visits = (n * TOP_K) // MOE_VISIT_ROWS + n_experts
    n_rows = max_visits * MOE_VISIT_ROWS
    visits_per = (counts + MOE_VISIT_ROWS - 1) // MOE_VISIT_ROWS
    visit_end = jnp.cumsum(visits_per)
    visit_start = visit_end - visits_per
    total = visit_end[-1]
    v_eff = jnp.minimum(jnp.arange(max_visits, dtype=jnp.int32), total - 1)
    vis_expert = jnp.minimum(jnp.searchsorted(visit_end, v_eff, side="right"), n_experts - 1).astype(jnp.int32)
    vis_rows = jnp.clip(counts[vis_expert] - (v_eff - visit_start[vis_expert]) * MOE_VISIT_ROWS, 0, MOE_VISIT_ROWS)
    vis_rows = jnp.where(jnp.arange(max_visits) < total, vis_rows, 0).astype(jnp.int32)
    pos = (visit_start[ids] * MOE_VISIT_ROWS + rank).astype(jnp.int32)
    pos_flat = pos.reshape(-1)
    token = jnp.broadcast_to(jnp.arange(n, dtype=jnp.int32)[:, None], (n, TOP_K)).reshape(-1)
    src_rows = jnp.zeros((n_rows,), jnp.int32).at[pos_flat].set(token)
    row_w = jnp.zeros((n_rows,), F32).at[pos_flat].set(weights.reshape(-1)).reshape(n_rows, 1)
    chunks_per_visit = MOE_VISIT_ROWS // MOE_CHUNK
    chunk_valid = (jnp.repeat(vis_rows, chunks_per_visit)
                   - jnp.tile(jnp.arange(chunks_per_visit, dtype=jnp.int32) * MOE_CHUNK, max_visits) > 0).astype(jnp.int32)

    xs = _dispatch(chunk_valid, src_rows, hn, n_rows)
    act = _gate_up(vis_expert, v_eff, vis_rows, xs, w_gate_up, b_gate_up)
    y_sorted = _down(vis_expert, v_eff, vis_rows, act, w_down, b_down, row_w)
    return _combine(pos_flat, h, y_sorted)


def kernel(x_prompt, x_sample, cache_k, cache_v, cache_mem_k, cache_mem_v, page_table, mem_prompt, rel_bias, norm1_g, w_in, q_norm_g, k_norm_g, gmlp_ln_g, gmlp_ln_b, gmlp_ws, gmlp_bs, mem_norm_g, w_mem_kv, mq_norm_g, mk_norm_g, w_br_moba, w_br_gmlp, w_br_mem, w_out, norm2_g, w_router, b_router, w_gate_up, b_gate_up, w_down, b_down):
    batch, s_len, d = x_prompt.shape
    n_seq, dec_len, _ = x_sample.shape
    depth, n_phys, page, heads, hd = cache_k.shape
    _, _, mem, mem_heads, mem_hd = cache_mem_k.shape
    assert depth == 1 and batch == 1 and dec_len == 1
    moba_w, mem_w, gw = heads * hd, mem_heads * mem_hd, gmlp_ln_g.shape[-1]
    n_pages = page_table.shape[1]
    past_len = n_pages * page
    assert MOBA_BLOCK % page == 0 and past_len % MOBA_BLOCK == 0 and s_len % MOBA_BLOCK == 0
    ppb = MOBA_BLOCK // page
    nb = past_len // MOBA_BLOCK
    assert nb >= MOBA_TOPK
    n = s_len + n_seq

    x_all = jnp.concatenate([x_prompt.reshape(s_len, d), x_sample.reshape(n_seq, d)], axis=0)
    w_in2 = w_in.reshape(d, -1)
    xn = _rmsnorm(x_all, norm1_g, BF16)
    off = 0
    q = _proj(xn, w_in2, off, moba_w, "headnorm", q_norm_g, hd); off += moba_w
    k = _proj(xn, w_in2, off, moba_w, "headnorm", k_norm_g, hd); off += moba_w
    v = _proj(xn, w_in2, off, moba_w); off += moba_w
    z = _proj(xn, w_in2, off, 2 * gw, "gelu"); off += 2 * gw
    mq = _proj(xn, w_in2, off, mem_w, "headnorm", mq_norm_g, mem_hd); off += mem_w
    gates = _proj(xn, w_in2, off, 3 * d, "sigmoid")

    bias_tab = _bias_tables(rel_bias)
    att_p = _moba_prompt(q, k, v, bias_tab, s_len, heads, hd)
    q_s3, k_s3, v_s3 = (a[s_len:].reshape(n_seq, 1, moba_w) for a in (q, k, v))
    cache_k3 = cache_k.reshape(n_phys, page, moba_w)
    cache_v3 = cache_v.reshape(n_phys, page, moba_w)
    pt_flat = page_table.reshape(-1)
    sel = _sample_gate(pt_flat, q_s3, cache_k3, n_seq, nb, ppb, heads, hd)
    sel_flat = sel[:, :, :MOBA_TOPK].reshape(-1)
    sample_bias = jnp.stack([bias_tab[:, 1, 0, :],
                             jnp.broadcast_to(bias_tab[:, 1, MOBA_BLOCK - 1, 0:1], (heads, MOBA_BLOCK)),
                             jnp.broadcast_to(bias_tab[:, 0, 0, 0:1], (heads, MOBA_BLOCK))], axis=1)
    att_s = _moba_sample(pt_flat, sel_flat, q_s3, k_s3, v_s3, sample_bias, cache_k3, cache_v3, n_seq, nb, ppb, heads, hd)
    att = jnp.concatenate([att_p, att_s.reshape(n_seq, moba_w)], axis=0)

    g_p, g_s, gv_s = _gmlp(z, s_len, gmlp_ln_g, gmlp_ln_b, gmlp_ws[0], gmlp_bs[0])
    g_out = jnp.concatenate([g_p, g_s], axis=0)

    mem_n = _rmsnorm(mem_prompt.reshape(mem, d), mem_norm_g, BF16)
    w_mem2 = w_mem_kv.reshape(d, 2 * mem_w)
    mk_p = _proj(mem_n, w_mem2, 0, mem_w, "headnorm", mk_norm_g, mem_hd)
    mv_p = _proj(mem_n, w_mem2, mem_w, mem_w)
    m_p = _mem_attn_prompt(mq, mk_p, mv_p, s_len, mem_heads)
    m_s = _mem_attn_sample(mq[s_len:].reshape(n_seq, 1, mem_w), cache_mem_k.reshape(n_seq, mem, mem_w),
                           cache_mem_v.reshape(n_seq, mem, mem_w), mem_heads)
    m_out = jnp.concatenate([m_p, m_s.reshape(n_seq, mem_w)], axis=0)

    mixed = _merge(att, g_out, m_out, gates, w_br_moba.reshape(moba_w, d), w_br_gmlp.reshape(gw, d), w_br_mem.reshape(mem_w, d))
    h = _proj(mixed, w_out.reshape(d, d), 0, d, "residual", x_all)
    n_experts = w_router.shape[-1]
    out = _moe(h, norm2_g, w_router.reshape(d, n_experts), b_router.reshape(n_experts),
               w_gate_up.reshape(n_experts, d, -1), b_gate_up.reshape(n_experts, -1),
               w_down.reshape(n_experts, -1, d), b_down.reshape(n_experts, d))

    y_prompt = out[:s_len].reshape(batch, s_len, d)
    y_sample = out[s_len:].reshape(n_seq, dec_len, d)
    k_prompt = k[:s_len].reshape(1, batch, s_len, heads, hd)
    v_prompt = v[:s_len].reshape(1, batch, s_len, heads, hd)
    mem_k_prompt = mk_p.reshape(1, batch, mem, mem_heads, mem_hd)
    mem_v_prompt = mv_p.reshape(1, batch, mem, mem_heads, mem_hd)
    k_sample = k[s_len:].reshape(1, n_seq, dec_len, heads, hd)
    v_sample = v[s_len:].reshape(1, n_seq, dec_len, heads, hd)
    gmlp_v_sample = gv_s.reshape(1, n_seq, dec_len, gmlp_ws.shape[1], -1)
    return (y_prompt, y_sample, k_prompt, v_prompt, mem_k_prompt, mem_v_prompt, k_sample, v_sample, gmlp_v_sample)
```

```python
import functools
import math

import numpy as np
import jax
import jax.numpy as jnp
from jax import lax
from jax.experimental import pallas as pl
from jax.experimental.pallas import tpu as pltpu

MOBA_BLOCK = 256
MOBA_TOPK = 3
REL_BUCKETS = 32
REL_MAX_DISTANCE = 128
TOP_K = 4
SWIGLU_LIMIT = 7.0
SWIGLU_ALPHA = 1.702
NORM_EPS = 1e-6

V7X_VMEM_BYTES = 64 * 1024 * 1024
VMEM_LIMIT = V7X_VMEM_BYTES - 12 * 1024 * 1024
LANES = 128

ROW_TILE = 1040
COL_TILE = 256
NORM_ROWS = 256
MOBA_SUPER = 4
GATE_BLOCKS = 4
MOE_CHUNK = 256
MOE_VISIT_ROWS = 1280
COMBINE_TOKENS = 32

F32 = jnp.float32
BF16 = jnp.bfloat16
HI = lax.Precision.HIGHEST
NT = (((1,), (1,)), ((), ()))
NEG_INF = float("-inf")
MASK_BIG = 2.0 ** 100


def _params(sem, vmem=VMEM_LIMIT):
    return pltpu.CompilerParams(dimension_semantics=sem, vmem_limit_bytes=vmem)


def _tile(n, pref):
    return n if n <= pref else pref


def _rmsnorm_kernel(x_ref, g_ref, o_ref):
    x = x_ref[...]
    y = x * lax.rsqrt(jnp.mean(x * x, axis=-1, keepdims=True) + NORM_EPS)
    o_ref[...] = (y * g_ref[...]).astype(o_ref.dtype)


def _rmsnorm(x, g, out_dtype):
    n, d = x.shape
    tm = _tile(n, NORM_ROWS)
    return pl.pallas_call(
        _rmsnorm_kernel,
        grid=(pl.cdiv(n, tm),),
        in_specs=[pl.BlockSpec((tm, d), lambda i: (i, 0)), pl.BlockSpec((1, d), lambda i: (0, 0))],
        out_specs=pl.BlockSpec((tm, d), lambda i: (i, 0)),
        out_shape=jax.ShapeDtypeStruct((n, d), out_dtype),
        compiler_params=_params(("parallel",)),
        name="rmsnorm",
    )(x, g.reshape(1, d))


def _proj_kernel(*refs, epilogue, head_dim):
    x_ref, w_ref, o_ref = refs[0], refs[1], refs[-1]
    acc = jnp.dot(x_ref[...], w_ref[...], preferred_element_type=F32)
    if epilogue == "headnorm":
        g = refs[2][...]
        for s in range(acc.shape[1] // head_dim):
            a = acc[:, s * head_dim:(s + 1) * head_dim]
            y = a * lax.rsqrt(jnp.mean(a * a, axis=-1, keepdims=True) + NORM_EPS)
            o_ref[:, s * head_dim:(s + 1) * head_dim] = (y * g).astype(o_ref.dtype)
    elif epilogue == "gelu":
        o_ref[...] = (acc * (lax.erf(acc * (1.0 / math.sqrt(2.0))) + 1.0) * 0.5).astype(o_ref.dtype)
    elif epilogue == "sigmoid":
        o_ref[...] = jax.nn.sigmoid(acc).astype(o_ref.dtype)
    elif epilogue == "residual":
        o_ref[...] = (refs[2][...] + acc).astype(o_ref.dtype)
    else:
        o_ref[...] = acc.astype(o_ref.dtype)


def _proj(x, w, col_off, width, epilogue="none", extra=None, head_dim=0, out_dtype=F32):
    n, k = x.shape
    tm = _tile(n, ROW_TILE)
    tn = _tile(width, COL_TILE)
    assert width % tn == 0 and col_off % tn == 0
    joff = col_off // tn
    in_specs = [pl.BlockSpec((tm, k), lambda i, j: (i, 0)), pl.BlockSpec((k, tn), lambda i, j: (0, joff + j))]
    args = [x, w]
    if epilogue == "headnorm":
        assert tn % head_dim == 0
        in_specs.append(pl.BlockSpec((1, head_dim), lambda i, j: (0, 0)))
        args.append(extra.reshape(1, head_dim))
    elif epilogue == "residual":
        in_specs.append(pl.BlockSpec((tm, tn), lambda i, j: (i, j)))
        args.append(extra)
    return pl.pallas_call(
        functools.partial(_proj_kernel, epilogue=epilogue, head_dim=head_dim),
        grid=(pl.cdiv(n, tm), width // tn),
        in_specs=in_specs,
        out_specs=pl.BlockSpec((tm, tn), lambda i, j: (i, j)),
        out_shape=jax.ShapeDtypeStruct((n, width), out_dtype),
        compiler_params=_params(("parallel", "arbitrary")),
        name="proj_" + epilogue,
    )(*args)


def _t5_bucket_np(dist):
    n = np.maximum(dist, 0)
    max_exact = REL_BUCKETS // 2
    log_ratio = (np.log(np.maximum(n, 1).astype(np.float32) / np.float32(max_exact))
                 / np.float32(math.log(REL_MAX_DISTANCE / max_exact)))
    large = max_exact + (log_ratio * np.float32(REL_BUCKETS - max_exact)).astype(np.int32)
    return np.where(n < max_exact, n, np.minimum(large, REL_BUCKETS - 1)).astype(np.int32)


def _bias_table_kernel(rel_ref, bucket_ref, o_ref):
    h = pl.program_id(0)
    for t in range(2):
        bucket = bucket_ref[t]
        acc = jnp.zeros(bucket.shape, F32)
        for b in range(REL_BUCKETS):
            acc = jnp.where(bucket == b, rel_ref[b, h], acc)
        o_ref[0, t] = acc - rel_ref[REL_BUCKETS - 1, h]


def _bias_tables(rel_bias):
    assert MOBA_BLOCK >= REL_MAX_DISTANCE
    heads = rel_bias.shape[1]
    i = np.arange(MOBA_BLOCK)[:, None]
    j = np.arange(MOBA_BLOCK)[None, :]
    buckets = jnp.asarray(np.stack([_t5_bucket_np(i - j), _t5_bucket_np(MOBA_BLOCK + i - j)]))
    return pl.pallas_call(
        _bias_table_kernel,
        grid=(heads,),
        in_specs=[pl.BlockSpec(memory_space=pltpu.SMEM),
                  pl.BlockSpec((2, MOBA_BLOCK, MOBA_BLOCK), lambda h: (0, 0, 0))],
        out_specs=pl.BlockSpec((1, 2, MOBA_BLOCK, MOBA_BLOCK), lambda h: (h, 0, 0, 0)),
        out_shape=jax.ShapeDtypeStruct((heads, 2, MOBA_BLOCK, MOBA_BLOCK), F32),
        compiler_params=_params(("arbitrary",)),
        name="bias_tables",
    )(rel_bias, buckets)


def _top_blocks(gate, blk_iota, n_blocks, axis):
    picked = []
    mask = jnp.zeros(gate.shape, jnp.bool_)
    g = gate
    for _ in range(MOBA_TOPK):
        m = jnp.max(g, axis=axis, keepdims=True)
        first = jnp.min(jnp.where((g == m) & (m > NEG_INF), blk_iota, n_blocks), axis=axis, keepdims=True)
        pick = blk_iota == first
        picked.append(first)
        mask = mask | pick
        g = jnp.where(pick, NEG_INF, g)
    return picked, mask


def _moba_prompt_kernel(q_ref, k_ref, v_ref, bias_ref, o_ref, kmean_ref, kaug_ref, vbf_ref, s_ref, *, scale):
    h, qb = pl.program_id(0), pl.program_id(1)
    blk = MOBA_BLOCK
    s_len, hd = k_ref.shape
    nb = s_len // blk

    @pl.when((h == 0) & (qb == 0))
    def _():
        key_blk = lax.broadcasted_iota(jnp.int32, (s_len, LANES), 0) // blk
        lane = lax.broadcasted_iota(jnp.int32, (s_len, LANES), 1)
        kaug_ref[:, hd:] = jnp.where(key_blk == lane, -MASK_BIG, 0.0).astype(BF16)

    @pl.when(qb == 0)
    def _():
        kmean_ref[...] = jnp.zeros(kmean_ref.shape, F32)
        kmean_ref[:nb, :] = jnp.mean(k_ref[...].reshape(nb, blk, hd), axis=1)
        kaug_ref[:, :hd] = k_ref[...].astype(BF16)
        vbf_ref[...] = v_ref[...].astype(BF16)

    q = q_ref[...]
    gate = lax.dot_general(q, kmean_ref[...], NT, precision=HI, preferred_element_type=F32)
    blk_iota = lax.broadcasted_iota(jnp.int32, (blk, LANES), 1)
    gate = jnp.where(blk_iota < qb, gate, NEG_INF)
    _, sel = _top_blocks(gate, blk_iota, LANES, axis=1)
    sel_f = sel.astype(F32)

    qs = (q * scale).astype(BF16)
    row = lax.broadcasted_iota(jnp.int32, (blk, blk), 0)
    col = lax.broadcasted_iota(jnp.int32, (blk, blk), 1)

    def logits(start, rows):
        return lax.dot_general(qs, kaug_ref[pl.ds(start, rows), :hd], NT, preferred_element_type=F32)

    def pv(p, start, rows):
        return jnp.dot(p.astype(BF16), vbf_ref[pl.ds(start, rows), :], preferred_element_type=F32)

    def chose(kb):
        return jnp.max(jnp.where(blk_iota == kb, sel_f, 0.0), axis=1, keepdims=True) > 0.0

    def update(carry, s, start, rows):
        m, l, acc = carry
        m_new = jnp.maximum(m, jnp.max(s, axis=1, keepdims=True))
        alpha = jnp.exp(m - m_new)
        p = jnp.exp(s - m_new)
        return m_new, alpha * l + jnp.sum(p, axis=1, keepdims=True), alpha * acc + pv(p, start, rows)

    own = pl.multiple_of(qb * blk, blk)
    s = jnp.where(row >= col, logits(own, blk) + bias_ref[0, 0], NEG_INF)
    m = jnp.max(s, axis=1, keepdims=True)
    p = jnp.exp(s - m)
    carry = (m, jnp.sum(p, axis=1, keepdims=True), pv(p, own, blk))

    prev_blk = jnp.maximum(qb - 1, 0)
    prev = pl.multiple_of(prev_blk * blk, blk)
    carry = update(carry, jnp.where(chose(prev_blk), logits(prev, blk) + bias_ref[0, 1], NEG_INF), prev, blk)

    blocked = jnp.where(sel & (blk_iota < qb - 1), 0.0, 1.0)
    q_aug = jnp.concatenate([qs, blocked.astype(BF16)], axis=1)
    span = MOBA_SUPER * blk

    def masked_logits(g):
        start = pl.multiple_of(g * span, span)
        return lax.dot_general(q_aug, kaug_ref[pl.ds(start, span), :], NT, preferred_element_type=F32)

    def body(g, carry):
        nxt = masked_logits(g + 1)
        carry = update(carry, s_ref[...], pl.multiple_of(g * span, span), span)
        s_ref[...] = nxt
        return carry

    last = jnp.maximum((jnp.maximum(qb - 1, 0) + MOBA_SUPER - 1) // MOBA_SUPER - 1, 0)
    s_ref[...] = masked_logits(0)
    carry = lax.fori_loop(0, last, body, carry)
    m, l, acc = update(carry, s_ref[...], pl.multiple_of(last * span, span), span)
    o_ref[...] = (acc / l).astype(o_ref.dtype)


def _moba_prompt(q, k, v, bias_tab, s_len, heads, hd):
    assert s_len % (MOBA_SUPER * MOBA_BLOCK) == 0 and s_len // MOBA_BLOCK <= LANES
    nqb = s_len // MOBA_BLOCK
    return pl.pallas_call(
        functools.partial(_moba_prompt_kernel, scale=hd ** -0.5),
        grid=(heads, nqb),
        in_specs=[pl.BlockSpec((MOBA_BLOCK, hd), lambda h, i: (i, h)),
                  pl.BlockSpec((s_len, hd), lambda h, i: (0, h)),
                  pl.BlockSpec((s_len, hd), lambda h, i: (0, h)),
                  pl.BlockSpec((1, 2, MOBA_BLOCK, MOBA_BLOCK), lambda h, i: (h, 0, 0, 0))],
        out_specs=pl.BlockSpec((MOBA_BLOCK, hd), lambda h, i: (i, h)),
        out_shape=jax.ShapeDtypeStruct((s_len, heads * hd), BF16),
        scratch_shapes=[pltpu.VMEM((LANES, hd), F32), pltpu.VMEM((s_len, hd + LANES), BF16),
                        pltpu.VMEM((s_len, hd), BF16), pltpu.VMEM((MOBA_BLOCK, MOBA_SUPER * MOBA_BLOCK), F32)],
        compiler_params=_params(("arbitrary", "arbitrary")),
        name="moba_prompt",
    )(q, k, v, bias_tab)


def _sample_gate_kernel(pt_ref, q_ref, *refs, ppb, bps):
    page_refs, sel_ref, kmean_ref = refs[:bps * ppb], refs[bps * ppb], refs[bps * ppb + 1]
    j = pl.program_id(1)
    nb, heads, _ = kmean_ref.shape
    for t in range(bps):
        total = jnp.sum(page_refs[t * ppb][0], axis=0)
        for p in range(1, ppb):
            total = total + jnp.sum(page_refs[t * ppb + p][0], axis=0)
        kmean_ref[j * bps + t] = total * (1.0 / MOBA_BLOCK)

    @pl.when(j == pl.num_programs(1) - 1)
    def _():
        gate = jnp.sum(kmean_ref[...] * q_ref[...], axis=-1)
        blk_iota = lax.broadcasted_iota(jnp.int32, (nb, heads), 0)
        picked, _ = _top_blocks(gate, blk_iota, nb, axis=0)
        sel_ref[...] = jnp.zeros(sel_ref.shape, jnp.int32)
        for t in range(MOBA_TOPK):
            sel_ref[0, t:t + 1, :heads] = picked[t]


def _sample_gate(page_table_flat, q_s, cache_k, n_seq, nb, ppb):
    _, page, heads, hd = cache_k.shape
    bps = _tile(nb, GATE_BLOCKS)
    assert nb % bps == 0 and heads <= LANES and MOBA_TOPK <= 8
    n_pages = nb * ppb
    page_specs = [pl.BlockSpec((1, page, heads, hd),
                               lambda b, j, pt, i=i: (pt[b * n_pages + j * bps * ppb + i], 0, 0, 0))
                  for i in range(bps * ppb)]
    return pl.pallas_call(
        functools.partial(_sample_gate_kernel, ppb=ppb, bps=bps),
        grid_spec=pltpu.PrefetchScalarGridSpec(
            num_scalar_prefetch=1,
            grid=(n_seq, nb // bps),
            in_specs=[pl.BlockSpec((1, heads, hd), lambda b, j, pt: (b, 0, 0))] + page_specs,
            out_specs=pl.BlockSpec((1, 8, LANES), lambda b, j, pt: (b, 0, 0)),
            scratch_shapes=[pltpu.VMEM((nb, heads, hd), F32)]),
        out_shape=jax.ShapeDtypeStruct((n_seq, 8, LANES), jnp.int32),
        compiler_params=_params(("parallel", "arbitrary")),
        name="moba_sample_gate",
    )(page_table_flat, q_s, *([cache_k] * (bps * ppb)))


def _moba_sample_kernel(pt_ref, sel_ref, q_ref, kn_ref, vn_ref, sb_ref, ck_ref, cv_ref, o_ref, kbuf, vbuf, sems,
                        *, nb, ppb, scale):
    b = pl.program_id(0)
    heads, n_keys, hd = kbuf.shape
    page = n_keys // (MOBA_TOPK * ppb)
    n_pages = nb * ppb

    def page_copies(h, t, p):
        phys = pt_ref[b * n_pages + sel_ref[(b * heads + h) * MOBA_TOPK + t] * ppb + p]
        dst = pl.ds((t * ppb + p) * page, page)
        return (pltpu.make_async_copy(ck_ref.at[phys, :, h, :], kbuf.at[h, dst, :], sems.at[0]),
                pltpu.make_async_copy(cv_ref.at[phys, :, h, :], vbuf.at[h, dst, :], sems.at[1]))

    slots = [(h, t, p) for h in range(heads) for t in range(MOBA_TOPK) for p in range(ppb)]
    for slot in slots:
        for copy in page_copies(*slot):
            copy.start()
    for slot in slots:
        for copy in page_copies(*slot):
            copy.wait()

    for h in range(heads):
        cols = slice(h * hd, (h + 1) * hd)
        q = q_ref[0, :, cols]
        q8 = jnp.broadcast_to(q, (8, hd)).astype(BF16)
        near, far, self_bias = sb_ref[h, 0:1, :], sb_ref[h, 1:2, :], sb_ref[h, 2:3, 0:1]
        bias = jnp.concatenate(
            [jnp.where(sel_ref[(b * heads + h) * MOBA_TOPK + t] == nb - 1, near, far) for t in range(MOBA_TOPK)], axis=1)
        s = lax.dot_general(q8, kbuf[h].astype(BF16), NT, preferred_element_type=F32)[0:1] * scale + bias
        s_self = jnp.sum(q * kn_ref[0, :, cols], axis=1, keepdims=True) * scale + self_bias
        m = jnp.maximum(jnp.max(s, axis=1, keepdims=True), s_self)
        p = jnp.exp(s - m)
        p_self = jnp.exp(s_self - m)
        l = jnp.sum(p, axis=1, keepdims=True) + p_self
        p8 = jnp.broadcast_to(p, (8, n_keys)).astype(BF16)
        acc = jnp.dot(p8, vbuf[h].astype(BF16), preferred_element_type=F32)[0:1] + p_self * vn_ref[0, :, cols]
        o_ref[0, :, cols] = (acc / l).astype(o_ref.dtype)


def _moba_sample(page_table_flat, sel_flat, q_s3, k_s3, v_s3, sample_bias, cache_k, cache_v, n_seq, nb, ppb):
    _, page, heads, hd = cache_k.shape
    n_keys = MOBA_TOPK * ppb * page
    row_spec = pl.BlockSpec((1, 1, heads * hd), lambda b, pt, sel: (b, 0, 0))
    return pl.pallas_call(
        functools.partial(_moba_sample_kernel, nb=nb, ppb=ppb, scale=hd ** -0.5),
        grid_spec=pltpu.PrefetchScalarGridSpec(
            num_scalar_prefetch=2,
            grid=(n_seq,),
            in_specs=[row_spec, row_spec, row_spec,
                      pl.BlockSpec((heads, 3, MOBA_BLOCK), lambda b, pt, sel: (0, 0, 0)),
                      pl.BlockSpec(memory_space=pl.ANY), pl.BlockSpec(memory_space=pl.ANY)],
            out_specs=row_spec,
            scratch_shapes=[pltpu.VMEM((heads, n_keys, hd), F32), pltpu.VMEM((heads, n_keys, hd), F32),
                            pltpu.SemaphoreType.DMA((2,))]),
        out_shape=jax.ShapeDtypeStruct((n_seq, 1, heads * hd), BF16),
        compiler_params=_params(("arbitrary",)),
        name="moba_sample",
    )(page_table_flat, sel_flat, q_s3, k_s3, v_s3, sample_bias, cache_k, cache_v)


def _layernorm(x, g, b):
    mu = jnp.mean(x, axis=-1, keepdims=True)
    var = jnp.mean(jnp.square(x - mu), axis=-1, keepdims=True)
    return (x - mu) * lax.rsqrt(var + NORM_EPS) * g + b


def _gmlp_prompt_kernel(z_ref, g_ref, b_ref, ws_ref, bs_ref, o_ref):
    groups, chunk, _ = ws_ref.shape
    gw = g_ref.shape[1]
    gd = gw // groups
    u = z_ref[:, :gw]
    gv = _layernorm(z_ref[:, gw:], g_ref[...], b_ref[...]).astype(BF16)
    row = lax.broadcasted_iota(jnp.int32, (chunk, chunk), 0)
    col = lax.broadcasted_iota(jnp.int32, (chunk, chunk), 1)
    for g in range(groups):
        ws = jnp.where(row >= col, ws_ref[g], 0.0).astype(BF16)
        mixed = jnp.dot(ws, gv[:, g * gd:(g + 1) * gd], preferred_element_type=F32) + bs_ref[:, g:g + 1]
        o_ref[:, g * gd:(g + 1) * gd] = (u[:, g * gd:(g + 1) * gd] * mixed).astype(o_ref.dtype)


def _gmlp_sample_kernel(z_ref, g_ref, b_ref, w0_ref, b0_ref, o_ref, gv_ref):
    gw = g_ref.shape[1]
    gv = _layernorm(z_ref[:, gw:], g_ref[...], b_ref[...])
    gv_ref[...] = gv
    o_ref[...] = (z_ref[:, :gw] * (w0_ref[...] * gv + b0_ref[...])).astype(o_ref.dtype)


def _gmlp(z, s_len, ln_g, ln_b, ws, bs):
    n, two_gw = z.shape
    gw = two_gw // 2
    groups, chunk, _ = ws.shape
    gd = gw // groups
    assert s_len % chunk == 0
    g2, b2 = ln_g.reshape(1, gw), ln_b.reshape(1, gw)
    out_p = pl.pallas_call(
        _gmlp_prompt_kernel,
        grid=(s_len // chunk,),
        in_specs=[pl.BlockSpec((chunk, two_gw), lambda c: (c, 0)),
                  pl.BlockSpec((1, gw), lambda c: (0, 0)), pl.BlockSpec((1, gw), lambda c: (0, 0)),
                  pl.BlockSpec((groups, chunk, chunk), lambda c: (0, 0, 0)),
                  pl.BlockSpec((chunk, groups), lambda c: (0, 0))],
        out_specs=pl.BlockSpec((chunk, gw), lambda c: (c, 0)),
        out_shape=jax.ShapeDtypeStruct((s_len, gw), BF16),
        compiler_params=_params(("parallel",)),
        name="gmlp_prompt",
    )(z, g2, b2, ws, bs.T)
    z_s = z[s_len:]
    n_s = n - s_len
    w0 = jnp.repeat(ws[:, 0, 0], gd).reshape(1, gw)
    b0 = jnp.repeat(bs[:, 0], gd).reshape(1, gw)
    out_s, gv_s = pl.pallas_call(
        _gmlp_sample_kernel,
        out_shape=(jax.ShapeDtypeStruct((n_s, gw), BF16), jax.ShapeDtypeStruct((n_s, gw), F32)),
        name="gmlp_sample",
    )(z_s, g2, b2, w0, b0)
    return out_p, out_s, gv_s


def _mem_attn_kernel(q_ref, k_ref, v_ref, o_ref, *, heads, scale, batched):
    q = q_ref[0] if batched else q_ref[...]
    k = k_ref[0] if batched else k_ref[...]
    v = v_ref[0] if batched else v_ref[...]
    rows = max(q.shape[0], 8)
    hd = q.shape[1] // heads
    for h in range(heads):
        sl = slice(h * hd, (h + 1) * hd)
        qh = jnp.broadcast_to(q[:, sl], (rows, hd)).astype(BF16)
        s = lax.dot_general(qh, k[:, sl].astype(BF16), NT, preferred_element_type=F32) * scale
        p = jnp.exp(s - jnp.max(s, axis=1, keepdims=True))
        l = jnp.sum(p, axis=1, keepdims=True)
        out = jnp.dot(p.astype(BF16), v[:, sl].astype(BF16), preferred_element_type=F32) / l
        out = out[:q.shape[0]].astype(o_ref.dtype)
        if batched:
            o_ref[0, :, sl] = out
        else:
            o_ref[:, sl] = out


def _mem_attn_prompt(mq, mk, mv, s_len, heads):
    width = mq.shape[1]
    mem = mk.shape[0]
    tq = _tile(s_len, 512)
    return pl.pallas_call(
        functools.partial(_mem_attn_kernel, heads=heads, scale=(width // heads) ** -0.5, batched=False),
        grid=(s_len // tq,),
        in_specs=[pl.BlockSpec((tq, width), lambda i: (i, 0)),
                  pl.BlockSpec((mem, width), lambda i: (0, 0)), pl.BlockSpec((mem, width), lambda i: (0, 0))],
        out_specs=pl.BlockSpec((tq, width), lambda i: (i, 0)),
        out_shape=jax.ShapeDtypeStruct((s_len, width), BF16),
        compiler_params=_params(("parallel",)),
        name="mem_attn_prompt",
    )(mq, mk, mv)


def _mem_attn_sample(mq_s3, ck, cv, heads):
    n_seq, mem, width = ck.shape
    return pl.pallas_call(
        functools.partial(_mem_attn_kernel, heads=heads, scale=(width // heads) ** -0.5, batched=True),
        grid=(n_seq,),
        in_specs=[pl.BlockSpec((1, 1, width), lambda b: (b, 0, 0)),
                  pl.BlockSpec((1, mem, width), lambda b: (b, 0, 0)), pl.BlockSpec((1, mem, width), lambda b: (b, 0, 0))],
        out_specs=pl.BlockSpec((1, 1, width), lambda b: (b, 0, 0)),
        out_shape=jax.ShapeDtypeStruct((n_seq, 1, width), BF16),
        compiler_params=_params(("parallel",)),
        name="mem_attn_sample",
    )(mq_s3, ck, cv)


def _merge_kernel(a_ref, g_ref, m_ref, ga_ref, gg_ref, gm_ref, wa_ref, wg_ref, wm_ref, o_ref):
    mixed = ga_ref[...] * jnp.dot(a_ref[...], wa_ref[...], preferred_element_type=F32)
    mixed = mixed + gg_ref[...] * jnp.dot(g_ref[...], wg_ref[...], preferred_element_type=F32)
    mixed = mixed + gm_ref[...] * jnp.dot(m_ref[...], wm_ref[...], preferred_element_type=F32)
    o_ref[...] = mixed.astype(o_ref.dtype)


def _merge(att, gout, mout, gates, w_a, w_g, w_m):
    n = att.shape[0]
    d = w_a.shape[1]
    tm = _tile(n, ROW_TILE)
    tn = _tile(d, COL_TILE)
    nj = d // tn

    def act_spec(a):
        return pl.BlockSpec((tm, a.shape[1]), lambda i, j: (i, 0))

    def gate_spec(branch):
        return pl.BlockSpec((tm, tn), lambda i, j: (i, branch * nj + j))

    def w_spec(w):
        return pl.BlockSpec((w.shape[0], tn), lambda i, j: (0, j))

    return pl.pallas_call(
        _merge_kernel,
        grid=(pl.cdiv(n, tm), nj),
        in_specs=[act_spec(att), act_spec(gout), act_spec(mout), gate_spec(0), gate_spec(1), gate_spec(2),
                  w_spec(w_a), w_spec(w_g), w_spec(w_m)],
        out_specs=pl.BlockSpec((tm, tn), lambda i, j: (i, j)),
        out_shape=jax.ShapeDtypeStruct((n, d), BF16),
        compiler_params=_params(("parallel", "arbitrary")),
        name="merge",
    )(att, gout, mout, gates, gates, gates, w_a, w_g, w_m)


def _router_kernel(h_ref, g_ref, wr_ref, br_ref, hn_ref, mi_ref, mw_ref, cnt_ref, carry_ref, *, n_tokens, n_experts):
    i = pl.program_id(0)
    tm = h_ref.shape[0]

    @pl.when(i == 0)
    def _():
        carry_ref[...] = jnp.zeros(carry_ref.shape, F32)

    x = h_ref[...]
    hn = x * lax.rsqrt(jnp.mean(x * x, axis=-1, keepdims=True) + NORM_EPS) * g_ref[...]
    hn_ref[...] = hn
    logits = jnp.dot(hn, wr_ref[...], precision=HI, preferred_element_type=F32) + br_ref[...]
    e_iota = lax.broadcasted_iota(jnp.int32, (tm, n_experts), 1)
    vals, ids = [], []
    l = logits
    for _ in range(TOP_K):
        m = jnp.max(l, axis=1, keepdims=True)
        first = jnp.min(jnp.where(l == m, e_iota, n_experts), axis=1, keepdims=True)
        vals.append(m)
        ids.append(first)
        l = jnp.where(e_iota == first, NEG_INF, l)
    ex = [jnp.exp(v - vals[0]) for v in vals]
    denom = ex[0]
    for e in ex[1:]:
        denom = denom + e
    valid = (i * tm + lax.broadcasted_iota(jnp.int32, (tm, 1), 0)) < n_tokens
    multi_hot = jnp.zeros((tm, n_experts), F32)
    for k in range(TOP_K):
        multi_hot = multi_hot + (e_iota == ids[k]).astype(F32)
    multi_hot = jnp.where(valid, multi_hot, 0.0)
    lower = (lax.broadcasted_iota(jnp.int32, (tm, tm), 0) > lax.broadcasted_iota(jnp.int32, (tm, tm), 1)).astype(BF16)
    before = jnp.dot(lower, multi_hot.astype(BF16), preferred_element_type=F32) + carry_ref[:, :n_experts]
    lane = lax.broadcasted_iota(jnp.int32, (tm, LANES), 1)
    meta_i = jnp.zeros((tm, LANES), jnp.int32)
    meta_w = jnp.zeros((tm, LANES), F32)
    for k in range(TOP_K):
        rank = jnp.sum(jnp.where(e_iota == ids[k], before, 0.0), axis=1, keepdims=True).astype(jnp.int32)
        meta_i = jnp.where(lane == k, ids[k], meta_i)
        meta_i = jnp.where(lane == TOP_K + k, rank, meta_i)
        meta_w = jnp.where(lane == k, ex[k] / denom, meta_w)
    mi_ref[...] = meta_i
    mw_ref[...] = meta_w
    carry_ref[:, :n_experts] = carry_ref[:, :n_experts] + jnp.sum(multi_hot, axis=0, keepdims=True)
    cnt_ref[...] = carry_ref[...].astype(jnp.int32)


def _router(h, g, w_router, b_router):
    n, d = h.shape
    n_experts = w_router.shape[1]
    assert n_experts <= LANES
    tm = _tile(n, NORM_ROWS)
    return pl.pallas_call(
        functools.partial(_router_kernel, n_tokens=n, n_experts=n_experts),
        grid=(pl.cdiv(n, tm),),
        in_specs=[pl.BlockSpec((tm, d), lambda i: (i, 0)), pl.BlockSpec((1, d), lambda i: (0, 0)),
                  pl.BlockSpec((d, n_experts), lambda i: (0, 0)), pl.BlockSpec((1, n_experts), lambda i: (0, 0))],
        out_specs=[pl.BlockSpec((tm, d), lambda i: (i, 0)), pl.BlockSpec((tm, LANES), lambda i: (i, 0)),
                   pl.BlockSpec((tm, LANES), lambda i: (i, 0)), pl.BlockSpec((1, LANES), lambda i: (0, 0))],
        out_shape=(jax.ShapeDtypeStruct((n, d), F32), jax.ShapeDtypeStruct((n, LANES), jnp.int32),
                   jax.ShapeDtypeStruct((n, LANES), F32), jax.ShapeDtypeStruct((1, LANES), jnp.int32)),
        scratch_shapes=[pltpu.VMEM((1, LANES), F32)],
        compiler_params=_params(("arbitrary",)),
        name="router",
    )(h, g.reshape(1, d), w_router, b_router.reshape(1, n_experts))


def _dispatch_kernel(valid_ref, src_ref, hn_ref, o_ref, buf_ref, sem):
    c = pl.program_id(0)
    rows = buf_ref.shape[0]

    def row_copy(r, src_row):
        return pltpu.make_async_copy(hn_ref.at[pl.ds(src_row, 1)], buf_ref.at[pl.ds(r, 1)], sem)

    @pl.when(c == 0)
    def _():
        buf_ref[...] = jnp.zeros(buf_ref.shape, buf_ref.dtype)

    @pl.when(valid_ref[c] > 0)
    def _():
        def start(r, carry):
            row_copy(r, src_ref[r]).start()
            return carry

        def wait(r, carry):
            row_copy(r, 0).wait()
            return carry

        lax.fori_loop(0, valid_ref[c], start, 0)
        lax.fori_loop(0, valid_ref[c], wait, 0)
        o_ref[...] = buf_ref[...].astype(o_ref.dtype)

    @pl.when(valid_ref[c] == 0)
    def _():
        o_ref[...] = jnp.zeros(o_ref.shape, o_ref.dtype)


def _dispatch(chunk_valid, src_rows, hn, n_rows):
    d = hn.shape[1]
    return pl.pallas_call(
        _dispatch_kernel,
        grid_spec=pltpu.PrefetchScalarGridSpec(
            num_scalar_prefetch=1,
            grid=(n_rows // MOE_CHUNK,),
            in_specs=[pl.BlockSpec((MOE_CHUNK,), lambda c, valid: (c,), memory_space=pltpu.SMEM),
                      pl.BlockSpec(memory_space=pl.ANY)],
            out_specs=pl.BlockSpec((MOE_CHUNK, d), lambda c, valid: (c, 0)),
            scratch_shapes=[pltpu.VMEM((MOE_CHUNK, d), F32), pltpu.SemaphoreType.DMA(())]),
        out_shape=jax.ShapeDtypeStruct((n_rows, d), BF16),
        compiler_params=_params(("arbitrary",)),
        name="moe_dispatch",
    )(chunk_valid, src_rows, hn)


def _visit_specs(n_col_tiles):
    def col(v, j, vr):
        return jnp.where(vr[v] > 0, j, n_col_tiles - 1)

    def rows(v, j, ve, vb, vr):
        return (vb[v], 0)

    def weight(v, j, ve, vb, vr):
        return (ve[v], 0, col(v, j, vr))

    def out(v, j, ve, vb, vr):
        return (v, j)

    return rows, weight, out


def _run_visit(n_rows, o_ref, compute, finish, stage_ref):
    half = MOE_CHUNK // 2
    full = n_rows // MOE_CHUNK
    rem = n_rows - full * MOE_CHUNK

    @pl.when(full > 0)
    def _():
        stage_ref[...] = compute(0, MOE_CHUNK)

        def body(c, carry):
            r0 = pl.multiple_of(c * MOE_CHUNK, MOE_CHUNK)
            nxt = compute(r0, MOE_CHUNK)
            finish(stage_ref[...], pl.multiple_of(r0 - MOE_CHUNK, MOE_CHUNK), MOE_CHUNK)
            stage_ref[...] = nxt
            return carry

        lax.fori_loop(1, full, body, 0)
        finish(stage_ref[...], pl.multiple_of((full - 1) * MOE_CHUNK, MOE_CHUNK), MOE_CHUNK)

    r_full = pl.multiple_of(full * MOE_CHUNK, MOE_CHUNK)

    @pl.when((rem > 0) & (rem <= half))
    def _():
        finish(compute(r_full, half), r_full, half)

    @pl.when(rem > half)
    def _():
        finish(compute(r_full, MOE_CHUNK), r_full, MOE_CHUNK)

    def zero(c, carry):
        o_ref[pl.ds(pl.multiple_of(c * half, half), half), :] = jnp.zeros((half, o_ref.shape[1]), o_ref.dtype)
        return carry

    done = 2 * full + jnp.where(rem > half, 2, jnp.where(rem > 0, 1, 0))
    lax.fori_loop(done, o_ref.shape[0] // half, zero, 0)


def _gate_up_kernel(ve_ref, vb_ref, vr_ref, x_ref, w_ref, b_ref, p_ref, o_ref, wb_ref, stage_ref):
    wb_ref[...] = w_ref[0].astype(BF16)
    tn = wb_ref.shape[1]

    def compute(r0, rows):
        return jnp.dot(x_ref[pl.ds(r0, rows), :], wb_ref[...], preferred_element_type=F32) + b_ref[0]

    def finish(gu, r0, rows):
        nxt = pltpu.roll(gu, tn - 1, 1)
        glu = jnp.minimum(gu, SWIGLU_LIMIT)
        lin = jnp.clip(nxt, -SWIGLU_LIMIT, SWIGLU_LIMIT)
        even = (lax.broadcasted_iota(jnp.int32, (rows, tn), 1) % 2) == 0
        act = jnp.where(even, glu * jax.nn.sigmoid(SWIGLU_ALPHA * glu) * (lin + 1.0), 0.0)
        o_ref[pl.ds(r0, rows), :] = jnp.dot(act.astype(BF16), p_ref[...], preferred_element_type=F32).astype(o_ref.dtype)

    _run_visit(vr_ref[pl.program_id(0)], o_ref, compute, finish, stage_ref)


def _gate_up(vis_expert, vis_block, vis_rows, xs, w_gate_up, b_gate_up):
    n_rows, d = xs.shape
    n_experts, _, two_f = w_gate_up.shape
    tn = _tile(two_f, COL_TILE)
    sel = np.zeros((tn, tn // 2), np.float32)
    sel[2 * np.arange(tn // 2), np.arange(tn // 2)] = 1.0
    rows, weight, out = _visit_specs(two_f // tn)
    return pl.pallas_call(
        _gate_up_kernel,
        grid_spec=pltpu.PrefetchScalarGridSpec(
            num_scalar_prefetch=3,
            grid=(n_rows // MOE_VISIT_ROWS, two_f // tn),
            in_specs=[pl.BlockSpec((MOE_VISIT_ROWS, d), rows),
                      pl.BlockSpec((1, d, tn), weight),
                      pl.BlockSpec((1, 1, tn), weight),
                      pl.BlockSpec((tn, tn // 2), lambda v, j, ve, vb, vr: (0, 0))],
            out_specs=pl.BlockSpec((MOE_VISIT_ROWS, tn // 2), out),
            scratch_shapes=[pltpu.VMEM((d, tn), BF16), pltpu.VMEM((MOE_CHUNK, tn), F32)]),
        out_shape=jax.ShapeDtypeStruct((n_rows, two_f // 2), BF16),
        compiler_params=_params(("arbitrary", "arbitrary")),
        name="moe_gate_up",
    )(vis_expert, vis_block, vis_rows, xs, w_gate_up, b_gate_up.reshape(n_experts, 1, two_f), jnp.asarray(sel, BF16))


def _down_kernel(ve_ref, vb_ref, vr_ref, x_ref, w_ref, b_ref, o_ref, wb_ref, stage_ref):
    wb_ref[...] = w_ref[0].astype(BF16)

    def compute(r0, rows):
        return jnp.dot(x_ref[pl.ds(r0, rows), :], wb_ref[...], preferred_element_type=F32) + b_ref[0]

    def finish(y, r0, rows):
        o_ref[pl.ds(r0, rows), :] = y

    _run_visit(vr_ref[pl.program_id(0)], o_ref, compute, finish, stage_ref)


def _down(vis_expert, vis_block, vis_rows, act, w_down, b_down):
    n_rows, f = act.shape
    n_experts, _, d = w_down.shape
    tn = _tile(d, COL_TILE)
    rows, weight, out = _visit_specs(d // tn)
    return pl.pallas_call(
        _down_kernel,
        grid_spec=pltpu.PrefetchScalarGridSpec(
            num_scalar_prefetch=3,
            grid=(n_rows // MOE_VISIT_ROWS, d // tn),
            in_specs=[pl.BlockSpec((MOE_VISIT_ROWS, f), rows),
                      pl.BlockSpec((1, f, tn), weight),
                      pl.BlockSpec((1, 1, tn), weight)],
            out_specs=pl.BlockSpec((MOE_VISIT_ROWS, tn), out),
            scratch_shapes=[pltpu.VMEM((f, tn), BF16), pltpu.VMEM((MOE_CHUNK, tn), F32)]),
        out_shape=jax.ShapeDtypeStruct((n_rows, d), F32),
        compiler_params=_params(("arbitrary", "arbitrary")),
        name="moe_down",
    )(vis_expert, vis_block, vis_rows, act, w_down, b_down.reshape(n_experts, 1, d))


def _combine_kernel(pos_ref, h_ref, w_ref, y_ref, o_ref, buf_ref, sem):
    tt = h_ref.shape[0]

    def row_copy(t, k, src_row):
        return pltpu.make_async_copy(y_ref.at[pl.ds(src_row, 1)], buf_ref.at[pl.ds(k * tt + t, 1)], sem)

    def start(t, carry):
        for k in range(TOP_K):
            row_copy(t, k, pos_ref[t * TOP_K + k]).start()
        return carry

    def wait(t, carry):
        for k in range(TOP_K):
            row_copy(t, k, 0).wait()
        return carry

    lax.fori_loop(0, tt, start, 0)
    lax.fori_loop(0, tt, wait, 0)
    out = h_ref[...]
    for k in range(TOP_K):
        out = out + w_ref[:, k:k + 1] * buf_ref[k * tt:(k + 1) * tt, :]
    o_ref[...] = out


def _combine(pos_flat, h, weights, y_sorted):
    n, d = h.shape
    tt = COMBINE_TOKENS
    n_tiles = pl.cdiv(n, tt)
    pos_flat = jnp.pad(pos_flat, (0, n_tiles * tt * TOP_K - pos_flat.shape[0]))
    return pl.pallas_call(
        _combine_kernel,
        grid=(n_tiles,),
        in_specs=[pl.BlockSpec((tt * TOP_K,), lambda i: (i,), memory_space=pltpu.SMEM),
                  pl.BlockSpec((tt, d), lambda i: (i, 0)),
                  pl.BlockSpec((tt, LANES), lambda i: (i, 0)),
                  pl.BlockSpec(memory_space=pl.ANY)],
        out_specs=pl.BlockSpec((tt, d), lambda i: (i, 0)),
        out_shape=jax.ShapeDtypeStruct((n, d), F32),
        scratch_shapes=[pltpu.VMEM((TOP_K * tt, d), F32), pltpu.SemaphoreType.DMA(())],
        compiler_params=_params(("arbitrary",)),
        name="moe_combine",
    )(pos_flat, h, weights, y_sorted)


def _moe(h, norm2_g, w_router, b_router, w_gate_up, b_gate_up, w_down, b_down):
    n, d = h.shape
    n_experts = w_router.shape[1]
    hn, meta_i, meta_w, counts = _router(h, norm2_g, w_router, b_router)
    ids, rank = meta_i[:, :TOP_K], meta_i[:, TOP_K:2 * TOP_K]
    counts = counts[0, :n_experts]

    max_visits = (n * TOP_K) // MOE_VISIT_ROWS + n_experts
    n_rows = max_visits * MOE_VISIT_ROWS
    visits_per = (counts + MOE_VISIT_ROWS - 1) // MOE_VISIT_ROWS
    visit_end = jnp.cumsum(visits_per)
    visit_start = visit_end - visits_per
    total = visit_end[-1]
    v_eff = jnp.minimum(jnp.arange(max_visits, dtype=jnp.int32), total - 1)
    vis_expert = jnp.minimum(jnp.searchsorted(visit_end, v_eff, side="right"), n_experts - 1).astype(jnp.int32)
    vis_rows = jnp.clip(counts[vis_expert] - (v_eff - visit_start[vis_expert]) * MOE_VISIT_ROWS, 0, MOE_VISIT_ROWS)
    vis_rows = jnp.where(jnp.arange(max_visits) < total, vis_rows, 0).astype(jnp.int32)
    pos = (visit_start[ids] * MOE_VISIT_ROWS + rank).astype(jnp.int32)
    pos_flat = pos.reshape(-1)
    token = jnp.broadcast_to(jnp.arange(n, dtype=jnp.int32)[:, None], (n, TOP_K)).reshape(-1)
    src_rows = jnp.zeros((n_rows,), jnp.int32).at[pos_flat].set(token)
    chunks_per_visit = MOE_VISIT_ROWS // MOE_CHUNK
    chunk_valid = jnp.clip(jnp.repeat(vis_rows, chunks_per_visit)
                           - jnp.tile(jnp.arange(chunks_per_visit, dtype=jnp.int32) * MOE_CHUNK, max_visits),
                           0, MOE_CHUNK).astype(jnp.int32)

    xs = _dispatch(chunk_valid, src_rows, hn, n_rows)
    act = _gate_up(vis_expert, v_eff, vis_rows, xs, w_gate_up, b_gate_up)
    y_sorted = _down(vis_expert, v_eff, vis_rows, act, w_down, b_down)
    return _combine(pos_flat, h, meta_w, y_sorted)


def kernel(x_prompt, x_sample, cache_k, cache_v, cache_mem_k, cache_mem_v, page_table, mem_prompt, rel_bias, norm1_g, w_in, q_norm_g, k_norm_g, gmlp_ln_g, gmlp_ln_b, gmlp_ws, gmlp_bs, mem_norm_g, w_mem_kv, mq_norm_g, mk_norm_g, w_br_moba, w_br_gmlp, w_br_mem, w_out, norm2_g, w_router, b_router, w_gate_up, b_gate_up, w_down, b_down):
    batch, s_len, d = x_prompt.shape
    n_seq, dec_len, _ = x_sample.shape
    depth, n_phys, page, heads, hd = cache_k.shape
    _, _, mem, mem_heads, mem_hd = cache_mem_k.shape
    assert depth == 1 and batch == 1 and dec_len == 1
    moba_w, mem_w, gw = heads * hd, mem_heads * mem_hd, gmlp_ln_g.shape[-1]
    n_pages = page_table.shape[1]
    past_len = n_pages * page
    assert MOBA_BLOCK % page == 0 and past_len % MOBA_BLOCK == 0 and s_len % MOBA_BLOCK == 0
    ppb = MOBA_BLOCK // page
    nb = past_len // MOBA_BLOCK
    assert nb >= MOBA_TOPK
    n = s_len + n_seq

    x_all = jnp.concatenate([x_prompt.reshape(s_len, d), x_sample.reshape(n_seq, d)], axis=0)
    w_in2 = w_in.reshape(d, -1)
    xn = _rmsnorm(x_all, norm1_g, BF16)
    off = 0
    q = _proj(xn, w_in2, off, moba_w, "headnorm", q_norm_g, hd); off += moba_w
    k = _proj(xn, w_in2, off, moba_w, "headnorm", k_norm_g, hd); off += moba_w
    v = _proj(xn, w_in2, off, moba_w); off += moba_w
    z = _proj(xn, w_in2, off, 2 * gw, "gelu"); off += 2 * gw
    mq = _proj(xn, w_in2, off, mem_w, "headnorm", mq_norm_g, mem_hd); off += mem_w
    gates = _proj(xn, w_in2, off, 3 * d, "sigmoid")

    bias_tab = _bias_tables(rel_bias)
    att_p = _moba_prompt(q, k, v, bias_tab, s_len, heads, hd)
    q_s3, k_s3, v_s3 = (a[s_len:].reshape(n_seq, 1, moba_w) for a in (q, k, v))
    cache_k4 = cache_k.reshape(n_phys, page, heads, hd)
    cache_v4 = cache_v.reshape(n_phys, page, heads, hd)
    pt_flat = page_table.reshape(-1)
    sel = _sample_gate(pt_flat, q_s3.reshape(n_seq, heads, hd), cache_k4, n_seq, nb, ppb)
    sel_flat = jnp.swapaxes(sel[:, :MOBA_TOPK, :heads], 1, 2).reshape(-1)
    sample_bias = jnp.stack([bias_tab[:, 1, 0, :],
                             jnp.broadcast_to(bias_tab[:, 1, MOBA_BLOCK - 1, 0:1], (heads, MOBA_BLOCK)),
                             jnp.broadcast_to(bias_tab[:, 0, 0, 0:1], (heads, MOBA_BLOCK))], axis=1)
    att_s = _moba_sample(pt_flat, sel_flat, q_s3, k_s3, v_s3, sample_bias, cache_k4, cache_v4, n_seq, nb, ppb)
    att = jnp.concatenate([att_p, att_s.reshape(n_seq, moba_w)], axis=0)

    g_p, g_s, gv_s = _gmlp(z, s_len, gmlp_ln_g, gmlp_ln_b, gmlp_ws[0], gmlp_bs[0])
    g_out = jnp.concatenate([g_p, g_s], axis=0)

    mem_n = _rmsnorm(mem_prompt.reshape(mem, d), mem_norm_g, BF16)
    w_mem2 = w_mem_kv.reshape(d, 2 * mem_w)
    mk_p = _proj(mem_n, w_mem2, 0, mem_w, "headnorm", mk_norm_g, mem_hd)
    mv_p = _proj(mem_n, w_mem2, mem_w, mem_w)
    m_p = _mem_attn_prompt(mq, mk_p, mv_p, s_len, mem_heads)
    m_s = _mem_attn_sample(mq[s_len:].reshape(n_seq, 1, mem_w), cache_mem_k.reshape(n_seq, mem, mem_w),
                           cache_mem_v.reshape(n_seq, mem, mem_w), mem_heads)
    m_out = jnp.concatenate([m_p, m_s.reshape(n_seq, mem_w)], axis=0)

    mixed = _merge(att, g_out, m_out, gates, w_br_moba.reshape(moba_w, d), w_br_gmlp.reshape(gw, d), w_br_mem.reshape(mem_w, d))
    h = _proj(mixed, w_out.reshape(d, d), 0, d, "residual", x_all)
    n_experts = w_router.shape[-1]
    out = _moe(h, norm2_g, w_router.reshape(d, n_experts), b_router.reshape(n_experts),
               w_gate_up.reshape(n_experts, d, -1), b_gate_up.reshape(n_experts, -1),
               w_down.reshape(n_experts, -1, d), b_down.reshape(n_experts, d))

    y_prompt = out[:s_len].reshape(batch, s_len, d)
    y_sample = out[s_len:].reshape(n_seq, dec_len, d)
    k_prompt = k[:s_len].reshape(1, batch, s_len, heads, hd)
    v_prompt = v[:s_len].reshape(1, batch, s_len, heads, hd)
    mem_k_prompt = mk_p.reshape(1, batch, mem, mem_heads, mem_hd)
    mem_v_prompt = mv_p.reshape(1, batch, mem, mem_heads, mem_hd)
    k_sample = k[s_len:].reshape(1, n_seq, dec_len, heads, hd)
    v_sample = v[s_len:].reshape(1, n_seq, dec_len, heads, hd)
    gmlp_v_sample = gv_s.reshape(1, n_seq, dec_len, gmlp_ws.shape[1], -1)
    return (y_prompt, y_sample, k_prompt, v_prompt, mem_k_prompt, mem_v_prompt, k_sample, v_sample, gmlp_v_sample)
```

```python
import functools
import math

import numpy as np
import jax
import jax.numpy as jnp
from jax import lax
from jax.experimental import pallas as pl
from jax.experimental.pallas import tpu as pltpu

MOBA_BLOCK = 256
MOBA_TOPK = 3
REL_BUCKETS = 32
REL_MAX_DISTANCE = 128
TOP_K = 4
SWIGLU_LIMIT = 7.0
SWIGLU_ALPHA = 1.702
NORM_EPS = 1e-6

V7X_VMEM_BYTES = 64 * 1024 * 1024
VMEM_LIMIT = V7X_VMEM_BYTES - 12 * 1024 * 1024
LANES = 128

ROW_TILE = 1040
COL_TILE = 256
PROJ_COL_TILE = 512
NORM_ROWS = 256
MOBA_SUPER = 4
GATE_BLOCKS = 4
MOE_CHUNK = 512
MOE_TAIL = 128
MOE_COL_TILE = 512
MOE_VISIT_ROWS = 1280
DISPATCH_ROWS = 256
COMBINE_TOKENS = 64

F32 = jnp.float32
BF16 = jnp.bfloat16
HI = lax.Precision.HIGHEST
NT = (((1,), (1,)), ((), ()))
NEG_INF = float("-inf")
MASK_BIG = 2.0 ** 100


def _params(sem, vmem=VMEM_LIMIT):
    return pltpu.CompilerParams(dimension_semantics=sem, vmem_limit_bytes=vmem)


def _tile(n, pref):
    return n if n <= pref else pref


def _rmsnorm_kernel(x_ref, g_ref, o_ref):
    x = x_ref[...]
    y = x * lax.rsqrt(jnp.mean(x * x, axis=-1, keepdims=True) + NORM_EPS)
    o_ref[...] = (y * g_ref[...]).astype(o_ref.dtype)


def _rmsnorm(x, g, out_dtype):
    n, d = x.shape
    tm = _tile(n, NORM_ROWS)
    return pl.pallas_call(
        _rmsnorm_kernel,
        grid=(pl.cdiv(n, tm),),
        in_specs=[pl.BlockSpec((tm, d), lambda i: (i, 0)), pl.BlockSpec((1, d), lambda i: (0, 0))],
        out_specs=pl.BlockSpec((tm, d), lambda i: (i, 0)),
        out_shape=jax.ShapeDtypeStruct((n, d), out_dtype),
        compiler_params=_params(("parallel",)),
        name="rmsnorm",
    )(x, g.reshape(1, d))


def _proj_kernel(*refs, epilogue, head_dim):
    x_ref, w_ref, o_ref = refs[0], refs[1], refs[-1]
    acc = jnp.dot(x_ref[...], w_ref[...], preferred_element_type=F32)
    if epilogue == "headnorm":
        g = refs[2][...]
        for s in range(acc.shape[1] // head_dim):
            a = acc[:, s * head_dim:(s + 1) * head_dim]
            y = a * lax.rsqrt(jnp.mean(a * a, axis=-1, keepdims=True) + NORM_EPS)
            o_ref[:, s * head_dim:(s + 1) * head_dim] = (y * g).astype(o_ref.dtype)
    elif epilogue == "gelu":
        o_ref[...] = (acc * (lax.erf(acc * (1.0 / math.sqrt(2.0))) + 1.0) * 0.5).astype(o_ref.dtype)
    elif epilogue == "sigmoid":
        o_ref[...] = jax.nn.sigmoid(acc).astype(o_ref.dtype)
    elif epilogue == "residual":
        o_ref[...] = (refs[2][...] + acc).astype(o_ref.dtype)
    else:
        o_ref[...] = acc.astype(o_ref.dtype)


def _proj(x, w, col_off, width, epilogue="none", extra=None, head_dim=0, out_dtype=F32):
    n, k = x.shape
    tm = _tile(n, ROW_TILE)
    tn = _tile(width, PROJ_COL_TILE)
    while width % tn or col_off % tn:
        tn //= 2
    assert tn % LANES == 0
    joff = col_off // tn
    in_specs = [pl.BlockSpec((tm, k), lambda i, j: (i, 0)), pl.BlockSpec((k, tn), lambda i, j: (0, joff + j))]
    args = [x, w]
    if epilogue == "headnorm":
        assert tn % head_dim == 0
        in_specs.append(pl.BlockSpec((1, head_dim), lambda i, j: (0, 0)))
        args.append(extra.reshape(1, head_dim))
    elif epilogue == "residual":
        in_specs.append(pl.BlockSpec((tm, tn), lambda i, j: (i, j)))
        args.append(extra)
    return pl.pallas_call(
        functools.partial(_proj_kernel, epilogue=epilogue, head_dim=head_dim),
        grid=(pl.cdiv(n, tm), width // tn),
        in_specs=in_specs,
        out_specs=pl.BlockSpec((tm, tn), lambda i, j: (i, j)),
        out_shape=jax.ShapeDtypeStruct((n, width), out_dtype),
        compiler_params=_params(("parallel", "arbitrary")),
        name="proj_" + epilogue,
    )(*args)


def _t5_bucket_np(dist):
    n = np.maximum(dist, 0)
    max_exact = REL_BUCKETS // 2
    log_ratio = (np.log(np.maximum(n, 1).astype(np.float32) / np.float32(max_exact))
                 / np.float32(math.log(REL_MAX_DISTANCE / max_exact)))
    large = max_exact + (log_ratio * np.float32(REL_BUCKETS - max_exact)).astype(np.int32)
    return np.where(n < max_exact, n, np.minimum(large, REL_BUCKETS - 1)).astype(np.int32)


def _bias_table_kernel(rel_ref, bucket_ref, o_ref):
    h = pl.program_id(0)
    for t in range(2):
        bucket = bucket_ref[t]
        acc = jnp.zeros(bucket.shape, F32)
        for b in range(REL_BUCKETS):
            acc = jnp.where(bucket == b, rel_ref[b, h], acc)
        o_ref[0, t] = acc - rel_ref[REL_BUCKETS - 1, h]


def _bias_tables(rel_bias):
    assert MOBA_BLOCK >= REL_MAX_DISTANCE
    heads = rel_bias.shape[1]
    i = np.arange(MOBA_BLOCK)[:, None]
    j = np.arange(MOBA_BLOCK)[None, :]
    buckets = jnp.asarray(np.stack([_t5_bucket_np(i - j), _t5_bucket_np(MOBA_BLOCK + i - j)]))
    return pl.pallas_call(
        _bias_table_kernel,
        grid=(heads,),
        in_specs=[pl.BlockSpec(memory_space=pltpu.SMEM),
                  pl.BlockSpec((2, MOBA_BLOCK, MOBA_BLOCK), lambda h: (0, 0, 0))],
        out_specs=pl.BlockSpec((1, 2, MOBA_BLOCK, MOBA_BLOCK), lambda h: (h, 0, 0, 0)),
        out_shape=jax.ShapeDtypeStruct((heads, 2, MOBA_BLOCK, MOBA_BLOCK), F32),
        compiler_params=_params(("arbitrary",)),
        name="bias_tables",
    )(rel_bias, buckets)


def _top_blocks(gate, blk_iota, n_blocks, axis):
    picked = []
    mask = jnp.zeros(gate.shape, jnp.bool_)
    g = gate
    for _ in range(MOBA_TOPK):
        m = jnp.max(g, axis=axis, keepdims=True)
        first = jnp.min(jnp.where((g == m) & (m > NEG_INF), blk_iota, n_blocks), axis=axis, keepdims=True)
        pick = blk_iota == first
        picked.append(first)
        mask = mask | pick
        g = jnp.where(pick, NEG_INF, g)
    return picked, mask


def _moba_prompt_kernel(q_ref, k_ref, v_ref, bias_ref, o_ref, kmean_ref, kaug_ref, vbf_ref, s_ref, *, scale):
    h, qb = pl.program_id(0), pl.program_id(1)
    blk = MOBA_BLOCK
    s_len, hd = k_ref.shape
    nb = s_len // blk

    @pl.when((h == 0) & (qb == 0))
    def _():
        key_blk = lax.broadcasted_iota(jnp.int32, (s_len, LANES), 0) // blk
        lane = lax.broadcasted_iota(jnp.int32, (s_len, LANES), 1)
        kaug_ref[:, hd:] = jnp.where(key_blk == lane, -MASK_BIG, 0.0).astype(BF16)

    @pl.when(qb == 0)
    def _():
        kmean_ref[...] = jnp.zeros(kmean_ref.shape, F32)
        kmean_ref[:nb, :] = jnp.mean(k_ref[...].reshape(nb, blk, hd), axis=1)
        kaug_ref[:, :hd] = k_ref[...].astype(BF16)
        vbf_ref[...] = v_ref[...].astype(BF16)

    q = q_ref[...]
    gate = lax.dot_general(q, kmean_ref[...], NT, precision=HI, preferred_element_type=F32)
    blk_iota = lax.broadcasted_iota(jnp.int32, (blk, LANES), 1)
    gate = jnp.where(blk_iota < qb, gate, NEG_INF)
    _, sel = _top_blocks(gate, blk_iota, LANES, axis=1)
    sel_f = sel.astype(F32)

    qs = (q * scale).astype(BF16)
    row = lax.broadcasted_iota(jnp.int32, (blk, blk), 0)
    col = lax.broadcasted_iota(jnp.int32, (blk, blk), 1)

    def logits(start, rows):
        return lax.dot_general(qs, kaug_ref[pl.ds(start, rows), :hd], NT, preferred_element_type=F32)

    def pv(p, start, rows):
        return jnp.dot(p.astype(BF16), vbf_ref[pl.ds(start, rows), :], preferred_element_type=F32)

    def chose(kb):
        return jnp.max(jnp.where(blk_iota == kb, sel_f, 0.0), axis=1, keepdims=True) > 0.0

    def update(carry, s, start, rows):
        m, l, acc = carry
        m_new = jnp.maximum(m, jnp.max(s, axis=1, keepdims=True))
        alpha = jnp.exp(m - m_new)
        p = jnp.exp(s - m_new)
        return m_new, alpha * l + jnp.sum(p, axis=1, keepdims=True), alpha * acc + pv(p, start, rows)

    own = pl.multiple_of(qb * blk, blk)
    s = jnp.where(row >= col, logits(own, blk) + bias_ref[0, 0], NEG_INF)
    m = jnp.max(s, axis=1, keepdims=True)
    p = jnp.exp(s - m)
    carry = (m, jnp.sum(p, axis=1, keepdims=True), pv(p, own, blk))

    prev_blk = jnp.maximum(qb - 1, 0)
    prev = pl.multiple_of(prev_blk * blk, blk)
    carry = update(carry, jnp.where(chose(prev_blk), logits(prev, blk) + bias_ref[0, 1], NEG_INF), prev, blk)

    blocked = jnp.where(sel & (blk_iota < qb - 1), 0.0, 1.0)
    q_aug = jnp.concatenate([qs, blocked.astype(BF16)], axis=1)
    span = MOBA_SUPER * blk

    def masked_logits(g):
        start = pl.multiple_of(g * span, span)
        return lax.dot_general(q_aug, kaug_ref[pl.ds(start, span), :], NT, preferred_element_type=F32)

    def body(g, carry):
        nxt = masked_logits(g + 1)
        carry = update(carry, s_ref[...], pl.multiple_of(g * span, span), span)
        s_ref[...] = nxt
        return carry

    last = jnp.maximum((jnp.maximum(qb - 1, 0) + MOBA_SUPER - 1) // MOBA_SUPER - 1, 0)
    s_ref[...] = masked_logits(0)
    carry = lax.fori_loop(0, last, body, carry)
    m, l, acc = update(carry, s_ref[...], pl.multiple_of(last * span, span), span)
    o_ref[...] = (acc / l).astype(o_ref.dtype)


def _moba_prompt(q, k, v, bias_tab, s_len, heads, hd):
    assert s_len % (MOBA_SUPER * MOBA_BLOCK) == 0 and s_len // MOBA_BLOCK <= LANES
    nqb = s_len // MOBA_BLOCK
    return pl.pallas_call(
        functools.partial(_moba_prompt_kernel, scale=hd ** -0.5),
        grid=(heads, nqb),
        in_specs=[pl.BlockSpec((MOBA_BLOCK, hd), lambda h, i: (i, h)),
                  pl.BlockSpec((s_len, hd), lambda h, i: (0, h)),
                  pl.BlockSpec((s_len, hd), lambda h, i: (0, h)),
                  pl.BlockSpec((1, 2, MOBA_BLOCK, MOBA_BLOCK), lambda h, i: (h, 0, 0, 0))],
        out_specs=pl.BlockSpec((MOBA_BLOCK, hd), lambda h, i: (i, h)),
        out_shape=jax.ShapeDtypeStruct((s_len, heads * hd), BF16),
        scratch_shapes=[pltpu.VMEM((LANES, hd), F32), pltpu.VMEM((s_len, hd + LANES), BF16),
                        pltpu.VMEM((s_len, hd), BF16), pltpu.VMEM((MOBA_BLOCK, MOBA_SUPER * MOBA_BLOCK), F32)],
        compiler_params=_params(("arbitrary", "arbitrary")),
        name="moba_prompt",
    )(q, k, v, bias_tab)


def _sample_gate_kernel(pt_ref, q_ref, *refs, ppb, bps):
    page_refs, sel_ref, kmean_ref = refs[:bps * ppb], refs[bps * ppb], refs[bps * ppb + 1]
    j = pl.program_id(1)
    nb, heads, _ = kmean_ref.shape
    for t in range(bps):
        total = jnp.sum(page_refs[t * ppb][0], axis=0)
        for p in range(1, ppb):
            total = total + jnp.sum(page_refs[t * ppb + p][0], axis=0)
        kmean_ref[j * bps + t] = total * (1.0 / MOBA_BLOCK)

    @pl.when(j == pl.num_programs(1) - 1)
    def _():
        gate = jnp.sum(kmean_ref[...] * q_ref[...], axis=-1)
        blk_iota = lax.broadcasted_iota(jnp.int32, (nb, heads), 0)
        picked, _ = _top_blocks(gate, blk_iota, nb, axis=0)
        sel_ref[...] = jnp.zeros(sel_ref.shape, jnp.int32)
        for t in range(MOBA_TOPK):
            sel_ref[0, t:t + 1, :heads] = picked[t]


def _sample_gate(page_table_flat, q_s, cache_k, n_seq, nb, ppb):
    _, page, heads, hd = cache_k.shape
    bps = _tile(nb, GATE_BLOCKS)
    assert nb % bps == 0 and heads <= LANES and MOBA_TOPK <= 8
    n_pages = nb * ppb
    page_specs = [pl.BlockSpec((1, page, heads, hd),
                               lambda b, j, pt, i=i: (pt[b * n_pages + j * bps * ppb + i], 0, 0, 0))
                  for i in range(bps * ppb)]
    return pl.pallas_call(
        functools.partial(_sample_gate_kernel, ppb=ppb, bps=bps),
        grid_spec=pltpu.PrefetchScalarGridSpec(
            num_scalar_prefetch=1,
            grid=(n_seq, nb // bps),
            in_specs=[pl.BlockSpec((1, heads, hd), lambda b, j, pt: (b, 0, 0))] + page_specs,
            out_specs=pl.BlockSpec((1, 8, LANES), lambda b, j, pt: (b, 0, 0)),
            scratch_shapes=[pltpu.VMEM((nb, heads, hd), F32)]),
        out_shape=jax.ShapeDtypeStruct((n_seq, 8, LANES), jnp.int32),
        compiler_params=_params(("parallel", "arbitrary")),
        name="moba_sample_gate",
    )(page_table_flat, q_s, *([cache_k] * (bps * ppb)))


def _moba_sample_kernel(pt_ref, sel_ref, q_ref, kn_ref, vn_ref, sb_ref, ck_ref, cv_ref, o_ref, kbuf, vbuf, sems,
                        *, nb, ppb, scale):
    b = pl.program_id(0)
    heads, n_keys, hd = kbuf.shape
    page = n_keys // (MOBA_TOPK * ppb)
    n_pages = nb * ppb

    def page_copies(h, t, p):
        phys = pt_ref[b * n_pages + sel_ref[(b * heads + h) * MOBA_TOPK + t] * ppb + p]
        dst = pl.ds((t * ppb + p) * page, page)
        return (pltpu.make_async_copy(ck_ref.at[phys, :, h, :], kbuf.at[h, dst, :], sems.at[0]),
                pltpu.make_async_copy(cv_ref.at[phys, :, h, :], vbuf.at[h, dst, :], sems.at[1]))

    slots = [(h, t, p) for h in range(heads) for t in range(MOBA_TOPK) for p in range(ppb)]
    for slot in slots:
        for copy in page_copies(*slot):
            copy.start()
    for slot in slots:
        for copy in page_copies(*slot):
            copy.wait()

    for h in range(heads):
        cols = slice(h * hd, (h + 1) * hd)
        q = q_ref[0, :, cols]
        q8 = jnp.broadcast_to(q, (8, hd)).astype(BF16)
        near, far, self_bias = sb_ref[h, 0:1, :], sb_ref[h, 1:2, :], sb_ref[h, 2:3, 0:1]
        bias = jnp.concatenate(
            [jnp.where(sel_ref[(b * heads + h) * MOBA_TOPK + t] == nb - 1, near, far) for t in range(MOBA_TOPK)], axis=1)
        s = lax.dot_general(q8, kbuf[h].astype(BF16), NT, preferred_element_type=F32)[0:1] * scale + bias
        s_self = jnp.sum(q * kn_ref[0, :, cols], axis=1, keepdims=True) * scale + self_bias
        m = jnp.maximum(jnp.max(s, axis=1, keepdims=True), s_self)
        p = jnp.exp(s - m)
        p_self = jnp.exp(s_self - m)
        l = jnp.sum(p, axis=1, keepdims=True) + p_self
        p8 = jnp.broadcast_to(p, (8, n_keys)).astype(BF16)
        acc = jnp.dot(p8, vbuf[h].astype(BF16), preferred_element_type=F32)[0:1] + p_self * vn_ref[0, :, cols]
        o_ref[0, :, cols] = (acc / l).astype(o_ref.dtype)


def _moba_sample(page_table_flat, sel_flat, q_s3, k_s3, v_s3, sample_bias, cache_k, cache_v, n_seq, nb, ppb):
    _, page, heads, hd = cache_k.shape
    n_keys = MOBA_TOPK * ppb * page
    row_spec = pl.BlockSpec((1, 1, heads * hd), lambda b, pt, sel: (b, 0, 0))
    return pl.pallas_call(
        functools.partial(_moba_sample_kernel, nb=nb, ppb=ppb, scale=hd ** -0.5),
        grid_spec=pltpu.PrefetchScalarGridSpec(
            num_scalar_prefetch=2,
            grid=(n_seq,),
            in_specs=[row_spec, row_spec, row_spec,
                      pl.BlockSpec((heads, 3, MOBA_BLOCK), lambda b, pt, sel: (0, 0, 0)),
                      pl.BlockSpec(memory_space=pl.ANY), pl.BlockSpec(memory_space=pl.ANY)],
            out_specs=row_spec,
            scratch_shapes=[pltpu.VMEM((heads, n_keys, hd), F32), pltpu.VMEM((heads, n_keys, hd), F32),
                            pltpu.SemaphoreType.DMA((2,))]),
        out_shape=jax.ShapeDtypeStruct((n_seq, 1, heads * hd), BF16),
        compiler_params=_params(("arbitrary",)),
        name="moba_sample",
    )(page_table_flat, sel_flat, q_s3, k_s3, v_s3, sample_bias, cache_k, cache_v)


def _layernorm(x, g, b):
    mu = jnp.mean(x, axis=-1, keepdims=True)
    var = jnp.mean(jnp.square(x - mu), axis=-1, keepdims=True)
    return (x - mu) * lax.rsqrt(var + NORM_EPS) * g + b


def _gmlp_prompt_kernel(z_ref, g_ref, b_ref, ws_ref, bs_ref, o_ref):
    groups, chunk, _ = ws_ref.shape
    gw = g_ref.shape[1]
    gd = gw // groups
    u = z_ref[:, :gw]
    gv = _layernorm(z_ref[:, gw:], g_ref[...], b_ref[...]).astype(BF16)
    row = lax.broadcasted_iota(jnp.int32, (chunk, chunk), 0)
    col = lax.broadcasted_iota(jnp.int32, (chunk, chunk), 1)
    for g in range(groups):
        ws = jnp.where(row >= col, ws_ref[g], 0.0).astype(BF16)
        mixed = jnp.dot(ws, gv[:, g * gd:(g + 1) * gd], preferred_element_type=F32) + bs_ref[:, g:g + 1]
        o_ref[:, g * gd:(g + 1) * gd] = (u[:, g * gd:(g + 1) * gd] * mixed).astype(o_ref.dtype)


def _gmlp_sample_kernel(z_ref, g_ref, b_ref, w0_ref, b0_ref, o_ref, gv_ref):
    gw = g_ref.shape[1]
    gv = _layernorm(z_ref[:, gw:], g_ref[...], b_ref[...])
    gv_ref[...] = gv
    o_ref[...] = (z_ref[:, :gw] * (w0_ref[...] * gv + b0_ref[...])).astype(o_ref.dtype)


def _gmlp(z, s_len, ln_g, ln_b, ws, bs):
    n, two_gw = z.shape
    gw = two_gw // 2
    groups, chunk, _ = ws.shape
    gd = gw // groups
    assert s_len % chunk == 0
    g2, b2 = ln_g.reshape(1, gw), ln_b.reshape(1, gw)
    out_p = pl.pallas_call(
        _gmlp_prompt_kernel,
        grid=(s_len // chunk,),
        in_specs=[pl.BlockSpec((chunk, two_gw), lambda c: (c, 0)),
                  pl.BlockSpec((1, gw), lambda c: (0, 0)), pl.BlockSpec((1, gw), lambda c: (0, 0)),
                  pl.BlockSpec((groups, chunk, chunk), lambda c: (0, 0, 0)),
                  pl.BlockSpec((chunk, groups), lambda c: (0, 0))],
        out_specs=pl.BlockSpec((chunk, gw), lambda c: (c, 0)),
        out_shape=jax.ShapeDtypeStruct((s_len, gw), BF16),
        compiler_params=_params(("parallel",)),
        name="gmlp_prompt",
    )(z, g2, b2, ws, bs.T)
    z_s = z[s_len:]
    n_s = n - s_len
    w0 = jnp.repeat(ws[:, 0, 0], gd).reshape(1, gw)
    b0 = jnp.repeat(bs[:, 0], gd).reshape(1, gw)
    out_s, gv_s = pl.pallas_call(
        _gmlp_sample_kernel,
        out_shape=(jax.ShapeDtypeStruct((n_s, gw), BF16), jax.ShapeDtypeStruct((n_s, gw), F32)),
        name="gmlp_sample",
    )(z_s, g2, b2, w0, b0)
    return out_p, out_s, gv_s


def _mem_attn_kernel(q_ref, k_ref, v_ref, o_ref, *, heads, scale, batched):
    q = q_ref[0] if batched else q_ref[...]
    k = k_ref[0] if batched else k_ref[...]
    v = v_ref[0] if batched else v_ref[...]
    rows = max(q.shape[0], 8)
    hd = q.shape[1] // heads
    for h in range(heads):
        sl = slice(h * hd, (h + 1) * hd)
        qh = jnp.broadcast_to(q[:, sl], (rows, hd)).astype(BF16)
        s = lax.dot_general(qh, k[:, sl].astype(BF16), NT, preferred_element_type=F32) * scale
        p = jnp.exp(s - jnp.max(s, axis=1, keepdims=True))
        l = jnp.sum(p, axis=1, keepdims=True)
        out = jnp.dot(p.astype(BF16), v[:, sl].astype(BF16), preferred_element_type=F32) / l
        out = out[:q.shape[0]].astype(o_ref.dtype)
        if batched:
            o_ref[0, :, sl] = out
        else:
            o_ref[:, sl] = out


def _mem_attn_prompt(mq, mk, mv, s_len, heads):
    width = mq.shape[1]
    mem = mk.shape[0]
    tq = _tile(s_len, 512)
    return pl.pallas_call(
        functools.partial(_mem_attn_kernel, heads=heads, scale=(width // heads) ** -0.5, batched=False),
        grid=(s_len // tq,),
        in_specs=[pl.BlockSpec((tq, width), lambda i: (i, 0)),
                  pl.BlockSpec((mem, width), lambda i: (0, 0)), pl.BlockSpec((mem, width), lambda i: (0, 0))],
        out_specs=pl.BlockSpec((tq, width), lambda i: (i, 0)),
        out_shape=jax.ShapeDtypeStruct((s_len, width), BF16),
        compiler_params=_params(("parallel",)),
        name="mem_attn_prompt",
    )(mq, mk, mv)


def _mem_attn_sample(mq_s3, ck, cv, heads):
    n_seq, mem, width = ck.shape
    return pl.pallas_call(
        functools.partial(_mem_attn_kernel, heads=heads, scale=(width // heads) ** -0.5, batched=True),
        grid=(n_seq,),
        in_specs=[pl.BlockSpec((1, 1, width), lambda b: (b, 0, 0)),
                  pl.BlockSpec((1, mem, width), lambda b: (b, 0, 0)), pl.BlockSpec((1, mem, width), lambda b: (b, 0, 0))],
        out_specs=pl.BlockSpec((1, 1, width), lambda b: (b, 0, 0)),
        out_shape=jax.ShapeDtypeStruct((n_seq, 1, width), BF16),
        compiler_params=_params(("parallel",)),
        name="mem_attn_sample",
    )(mq_s3, ck, cv)


def _merge_kernel(a_ref, g_ref, m_ref, ga_ref, gg_ref, gm_ref, wa_ref, wg_ref, wm_ref, o_ref):
    mixed = ga_ref[...] * jnp.dot(a_ref[...], wa_ref[...], preferred_element_type=F32)
    mixed = mixed + gg_ref[...] * jnp.dot(g_ref[...], wg_ref[...], preferred_element_type=F32)
    mixed = mixed + gm_ref[...] * jnp.dot(m_ref[...], wm_ref[...], preferred_element_type=F32)
    o_ref[...] = mixed.astype(o_ref.dtype)


def _merge(att, gout, mout, gates, w_a, w_g, w_m):
    n = att.shape[0]
    d = w_a.shape[1]
    tm = _tile(n, ROW_TILE)
    tn = _tile(d, COL_TILE)
    nj = d // tn

    def act_spec(a):
        return pl.BlockSpec((tm, a.shape[1]), lambda i, j: (i, 0))

    def gate_spec(branch):
        return pl.BlockSpec((tm, tn), lambda i, j: (i, branch * nj + j))

    def w_spec(w):
        return pl.BlockSpec((w.shape[0], tn), lambda i, j: (0, j))

    return pl.pallas_call(
        _merge_kernel,
        grid=(pl.cdiv(n, tm), nj),
        in_specs=[act_spec(att), act_spec(gout), act_spec(mout), gate_spec(0), gate_spec(1), gate_spec(2),
                  w_spec(w_a), w_spec(w_g), w_spec(w_m)],
        out_specs=pl.BlockSpec((tm, tn), lambda i, j: (i, j)),
        out_shape=jax.ShapeDtypeStruct((n, d), BF16),
        compiler_params=_params(("parallel", "arbitrary")),
        name="merge",
    )(att, gout, mout, gates, gates, gates, w_a, w_g, w_m)


def _router_kernel(h_ref, g_ref, wr_ref, br_ref, hn_ref, mi_ref, mw_ref, cnt_ref, carry_ref, *, n_tokens, n_experts):
    i = pl.program_id(0)
    tm = h_ref.shape[0]

    @pl.when(i == 0)
    def _():
        carry_ref[...] = jnp.zeros(carry_ref.shape, F32)

    x = h_ref[...]
    hn = x * lax.rsqrt(jnp.mean(x * x, axis=-1, keepdims=True) + NORM_EPS) * g_ref[...]
    hn_ref[...] = hn
    logits = jnp.dot(hn, wr_ref[...], precision=HI, preferred_element_type=F32) + br_ref[...]
    e_iota = lax.broadcasted_iota(jnp.int32, (tm, n_experts), 1)
    vals, ids = [], []
    l = logits
    for _ in range(TOP_K):
        m = jnp.max(l, axis=1, keepdims=True)
        first = jnp.min(jnp.where(l == m, e_iota, n_experts), axis=1, keepdims=True)
        vals.append(m)
        ids.append(first)
        l = jnp.where(e_iota == first, NEG_INF, l)
    ex = [jnp.exp(v - vals[0]) for v in vals]
    denom = ex[0]
    for e in ex[1:]:
        denom = denom + e
    valid = (i * tm + lax.broadcasted_iota(jnp.int32, (tm, 1), 0)) < n_tokens
    multi_hot = jnp.zeros((tm, n_experts), F32)
    for k in range(TOP_K):
        multi_hot = multi_hot + (e_iota == ids[k]).astype(F32)
    multi_hot = jnp.where(valid, multi_hot, 0.0)
    lower = (lax.broadcasted_iota(jnp.int32, (tm, tm), 0) > lax.broadcasted_iota(jnp.int32, (tm, tm), 1)).astype(BF16)
    before = jnp.dot(lower, multi_hot.astype(BF16), preferred_element_type=F32) + carry_ref[:, :n_experts]
    lane = lax.broadcasted_iota(jnp.int32, (tm, LANES), 1)
    meta_i = jnp.zeros((tm, LANES), jnp.int32)
    meta_w = jnp.zeros((tm, LANES), F32)
    for k in range(TOP_K):
        rank = jnp.sum(jnp.where(e_iota == ids[k], before, 0.0), axis=1, keepdims=True).astype(jnp.int32)
        meta_i = jnp.where(lane == k, ids[k], meta_i)
        meta_i = jnp.where(lane == TOP_K + k, rank, meta_i)
        meta_w = jnp.where(lane == k, ex[k] / denom, meta_w)
    mi_ref[...] = meta_i
    mw_ref[...] = meta_w
    carry_ref[:, :n_experts] = carry_ref[:, :n_experts] + jnp.sum(multi_hot, axis=0, keepdims=True)
    cnt_ref[...] = carry_ref[...].astype(jnp.int32)


def _router(h, g, w_router, b_router):
    n, d = h.shape
    n_experts = w_router.shape[1]
    assert n_experts <= LANES
    tm = _tile(n, NORM_ROWS)
    return pl.pallas_call(
        functools.partial(_router_kernel, n_tokens=n, n_experts=n_experts),
        grid=(pl.cdiv(n, tm),),
        in_specs=[pl.BlockSpec((tm, d), lambda i: (i, 0)), pl.BlockSpec((1, d), lambda i: (0, 0)),
                  pl.BlockSpec((d, n_experts), lambda i: (0, 0)), pl.BlockSpec((1, n_experts), lambda i: (0, 0))],
        out_specs=[pl.BlockSpec((tm, d), lambda i: (i, 0)), pl.BlockSpec((tm, LANES), lambda i: (i, 0)),
                   pl.BlockSpec((tm, LANES), lambda i: (i, 0)), pl.BlockSpec((1, LANES), lambda i: (0, 0))],
        out_shape=(jax.ShapeDtypeStruct((n, d), F32), jax.ShapeDtypeStruct((n, LANES), jnp.int32),
                   jax.ShapeDtypeStruct((n, LANES), F32), jax.ShapeDtypeStruct((1, LANES), jnp.int32)),
        scratch_shapes=[pltpu.VMEM((1, LANES), F32)],
        compiler_params=_params(("arbitrary",)),
        name="router",
    )(h, g.reshape(1, d), w_router, b_router.reshape(1, n_experts))


def _dispatch_kernel(valid_ref, src_ref, hn_ref, o_ref, buf_ref, sem):
    c = pl.program_id(0)
    rows = buf_ref.shape[0]

    def row_copy(r, src_row):
        return pltpu.make_async_copy(hn_ref.at[pl.ds(src_row, 1)], buf_ref.at[pl.ds(r, 1)], sem)

    @pl.when(c == 0)
    def _():
        buf_ref[...] = jnp.zeros(buf_ref.shape, buf_ref.dtype)

    @pl.when(valid_ref[c] > 0)
    def _():
        def start(r, carry):
            row_copy(r, src_ref[r]).start()
            return carry

        def wait(r, carry):
            row_copy(r, 0).wait()
            return carry

        lax.fori_loop(0, valid_ref[c], start, 0)
        lax.fori_loop(0, valid_ref[c], wait, 0)
        o_ref[...] = buf_ref[...].astype(o_ref.dtype)

    @pl.when(valid_ref[c] == 0)
    def _():
        o_ref[...] = jnp.zeros(o_ref.shape, o_ref.dtype)


def _dispatch(chunk_valid, src_rows, hn, n_rows):
    d = hn.shape[1]
    return pl.pallas_call(
        _dispatch_kernel,
        grid_spec=pltpu.PrefetchScalarGridSpec(
            num_scalar_prefetch=1,
            grid=(n_rows // DISPATCH_ROWS,),
            in_specs=[pl.BlockSpec((DISPATCH_ROWS,), lambda c, valid: (c,), memory_space=pltpu.SMEM),
                      pl.BlockSpec(memory_space=pl.ANY)],
            out_specs=pl.BlockSpec((DISPATCH_ROWS, d), lambda c, valid: (c, 0)),
            scratch_shapes=[pltpu.VMEM((DISPATCH_ROWS, d), F32), pltpu.SemaphoreType.DMA(())]),
        out_shape=jax.ShapeDtypeStruct((n_rows, d), BF16),
        compiler_params=_params(("arbitrary",)),
        name="moe_dispatch",
    )(chunk_valid, src_rows, hn)


def _visit_specs(n_col_tiles):
    def col(v, j, vr):
        return jnp.where(vr[v] > 0, j, n_col_tiles - 1)

    def rows(v, j, ve, vb, vr):
        return (vb[v], 0)

    def weight(v, j, ve, vb, vr):
        return (ve[v], 0, col(v, j, vr))

    def out(v, j, ve, vb, vr):
        return (v, j)

    return rows, weight, out


def _run_visit(n_rows, o_ref, compute, finish, stage_ref):
    units_per_chunk = MOE_CHUNK // MOE_TAIL
    full = n_rows // MOE_CHUNK
    tail_units = (n_rows - full * MOE_CHUNK + MOE_TAIL - 1) // MOE_TAIL
    merge = (full > 0) & (tail_units == 1)
    full = jnp.where(merge, full - 1, full)
    tail_units = jnp.where(merge, units_per_chunk + 1, tail_units)

    @pl.when(full > 0)
    def _():
        stage_ref[...] = compute(0, MOE_CHUNK)

        def body(c, carry):
            r0 = pl.multiple_of(c * MOE_CHUNK, MOE_CHUNK)
            nxt = compute(r0, MOE_CHUNK)
            finish(stage_ref[...], pl.multiple_of(r0 - MOE_CHUNK, MOE_CHUNK), MOE_CHUNK)
            stage_ref[...] = nxt
            return carry

        lax.fori_loop(1, full, body, 0)
        finish(stage_ref[...], pl.multiple_of((full - 1) * MOE_CHUNK, MOE_CHUNK), MOE_CHUNK)

    r_full = pl.multiple_of(full * MOE_CHUNK, MOE_CHUNK)
    for units in range(1, units_per_chunk + 2):
        @pl.when(tail_units == units)
        def _(rows=units * MOE_TAIL):
            finish(compute(r_full, rows), r_full, rows)

    def zero(c, carry):
        o_ref[pl.ds(pl.multiple_of(c * MOE_TAIL, MOE_TAIL), MOE_TAIL), :] = jnp.zeros((MOE_TAIL, o_ref.shape[1]), o_ref.dtype)
        return carry

    lax.fori_loop(full * units_per_chunk + tail_units, o_ref.shape[0] // MOE_TAIL, zero, 0)


def _gate_up_kernel(ve_ref, vb_ref, vr_ref, x_ref, w_ref, b_ref, p_ref, o_ref, wb_ref, stage_ref):
    @pl.when(vr_ref[pl.program_id(0)] > 0)
    def _():
        wb_ref[...] = w_ref[0].astype(BF16)

    tn = wb_ref.shape[1]

    def compute(r0, rows):
        return jnp.dot(x_ref[pl.ds(r0, rows), :], wb_ref[...], preferred_element_type=F32) + b_ref[0]

    def finish(gu, r0, rows):
        nxt = pltpu.roll(gu, tn - 1, 1)
        glu = jnp.minimum(gu, SWIGLU_LIMIT)
        lin = jnp.clip(nxt, -SWIGLU_LIMIT, SWIGLU_LIMIT)
        even = (lax.broadcasted_iota(jnp.int32, (rows, tn), 1) % 2) == 0
        act = jnp.where(even, glu * jax.nn.sigmoid(SWIGLU_ALPHA * glu) * (lin + 1.0), 0.0)
        o_ref[pl.ds(r0, rows), :] = jnp.dot(act.astype(BF16), p_ref[...], preferred_element_type=F32).astype(o_ref.dtype)

    _run_visit(vr_ref[pl.program_id(0)], o_ref, compute, finish, stage_ref)


def _gate_up(vis_expert, vis_block, vis_rows, xs, w_gate_up, b_gate_up):
    n_rows, d = xs.shape
    n_experts, _, two_f = w_gate_up.shape
    tn = _tile(two_f, MOE_COL_TILE)
    sel = np.zeros((tn, tn // 2), np.float32)
    sel[2 * np.arange(tn // 2), np.arange(tn // 2)] = 1.0
    rows, weight, out = _visit_specs(two_f // tn)
    return pl.pallas_call(
        _gate_up_kernel,
        grid_spec=pltpu.PrefetchScalarGridSpec(
            num_scalar_prefetch=3,
            grid=(n_rows // MOE_VISIT_ROWS, two_f // tn),
            in_specs=[pl.BlockSpec((MOE_VISIT_ROWS, d), rows),
                      pl.BlockSpec((1, d, tn), weight),
                      pl.BlockSpec((1, 1, tn), weight),
                      pl.BlockSpec((tn, tn // 2), lambda v, j, ve, vb, vr: (0, 0))],
            out_specs=pl.BlockSpec((MOE_VISIT_ROWS, tn // 2), out),
            scratch_shapes=[pltpu.VMEM((d, tn), BF16), pltpu.VMEM((MOE_CHUNK, tn), F32)]),
        out_shape=jax.ShapeDtypeStruct((n_rows, two_f // 2), BF16),
        compiler_params=_params(("arbitrary", "arbitrary")),
        name="moe_gate_up",
    )(vis_expert, vis_block, vis_rows, xs, w_gate_up, b_gate_up.reshape(n_experts, 1, two_f), jnp.asarray(sel, BF16))


def _down_kernel(ve_ref, vb_ref, vr_ref, x_ref, w_ref, b_ref, o_ref, wb_ref, stage_ref):
    @pl.when(vr_ref[pl.program_id(0)] > 0)
    def _():
        wb_ref[...] = w_ref[0].astype(BF16)

    def compute(r0, rows):
        return jnp.dot(x_ref[pl.ds(r0, rows), :], wb_ref[...], preferred_element_type=F32) + b_ref[0]

    def finish(y, r0, rows):
        o_ref[pl.ds(r0, rows), :] = y

    _run_visit(vr_ref[pl.program_id(0)], o_ref, compute, finish, stage_ref)


def _down(vis_expert, vis_block, vis_rows, act, w_down, b_down):
    n_rows, f = act.shape
    n_experts, _, d = w_down.shape
    tn = _tile(d, MOE_COL_TILE)
    rows, weight, out = _visit_specs(d // tn)
    return pl.pallas_call(
        _down_kernel,
        grid_spec=pltpu.PrefetchScalarGridSpec(
            num_scalar_prefetch=3,
            grid=(n_rows // MOE_VISIT_ROWS, d // tn),
            in_specs=[pl.BlockSpec((MOE_VISIT_ROWS, f), rows),
                      pl.BlockSpec((1, f, tn), weight),
                      pl.BlockSpec((1, 1, tn), weight)],
            out_specs=pl.BlockSpec((MOE_VISIT_ROWS, tn), out),
            scratch_shapes=[pltpu.VMEM((f, tn), BF16), pltpu.VMEM((MOE_CHUNK, tn), F32)]),
        out_shape=jax.ShapeDtypeStruct((n_rows, d), F32),
        compiler_params=_params(("arbitrary", "arbitrary")),
        name="moe_down",
    )(vis_expert, vis_block, vis_rows, act, w_down, b_down.reshape(n_experts, 1, d))


def _combine_kernel(pos_ref, h_ref, w_ref, y_ref, o_ref, buf_ref, sem):
    tt = h_ref.shape[0]

    def row_copy(t, k, src_row):
        return pltpu.make_async_copy(y_ref.at[pl.ds(src_row, 1)], buf_ref.at[pl.ds(k * tt + t, 1)], sem)

    def start(t, carry):
        for k in range(TOP_K):
            row_copy(t, k, pos_ref[t * TOP_K + k]).start()
        return carry

    def wait(t, carry):
        for k in range(TOP_K):
            row_copy(t, k, 0).wait()
        return carry

    lax.fori_loop(0, tt, start, 0)
    lax.fori_loop(0, tt, wait, 0)
    out = h_ref[...]
    for k in range(TOP_K):
        out = out + w_ref[:, k:k + 1] * buf_ref[k * tt:(k + 1) * tt, :]
    o_ref[...] = out


def _combine(pos_flat, h, weights, y_sorted):
    n, d = h.shape
    tt = COMBINE_TOKENS
    n_tiles = pl.cdiv(n, tt)
    pos_flat = jnp.pad(pos_flat, (0, n_tiles * tt * TOP_K - pos_flat.shape[0]))
    return pl.pallas_call(
        _combine_kernel,
        grid=(n_tiles,),
        in_specs=[pl.BlockSpec((tt * TOP_K,), lambda i: (i,), memory_space=pltpu.SMEM),
                  pl.BlockSpec((tt, d), lambda i: (i, 0)),
                  pl.BlockSpec((tt, LANES), lambda i: (i, 0)),
                  pl.BlockSpec(memory_space=pl.ANY)],
        out_specs=pl.BlockSpec((tt, d), lambda i: (i, 0)),
        out_shape=jax.ShapeDtypeStruct((n, d), F32),
        scratch_shapes=[pltpu.VMEM((TOP_K * tt, d), F32), pltpu.SemaphoreType.DMA(())],
        compiler_params=_params(("arbitrary",)),
        name="moe_combine",
    )(pos_flat, h, weights, y_sorted)


def _moe(h, norm2_g, w_router, b_router, w_gate_up, b_gate_up, w_down, b_down):
    n, d = h.shape
    n_experts = w_router.shape[1]
    hn, meta_i, meta_w, counts = _router(h, norm2_g, w_router, b_router)
    ids, rank = meta_i[:, :TOP_K], meta_i[:, TOP_K:2 * TOP_K]
    counts = counts[0, :n_experts]

    max_visits = (n * TOP_K) // MOE_VISIT_ROWS + n_experts
    n_rows = max_visits * MOE_VISIT_ROWS
    visits_per = (counts + MOE_VISIT_ROWS - 1) // MOE_VISIT_ROWS
    visit_end = jnp.cumsum(visits_per)
    visit_start = visit_end - visits_per
    total = visit_end[-1]
    v_eff = jnp.minimum(jnp.arange(max_visits, dtype=jnp.int32), total - 1)
    vis_expert = jnp.minimum(jnp.searchsorted(visit_end, v_eff, side="right"), n_experts - 1).astype(jnp.int32)
    vis_rows = jnp.clip(counts[vis_expert] - (v_eff - visit_start[vis_expert]) * MOE_VISIT_ROWS, 0, MOE_VISIT_ROWS)
    vis_rows = jnp.where(jnp.arange(max_visits) < total, vis_rows, 0).astype(jnp.int32)
    pos = (visit_start[ids] * MOE_VISIT_ROWS + rank).astype(jnp.int32)
    pos_flat = pos.reshape(-1)
    token = jnp.broadcast_to(jnp.arange(n, dtype=jnp.int32)[:, None], (n, TOP_K)).reshape(-1)
    src_rows = jnp.zeros((n_rows,), jnp.int32).at[pos_flat].set(token)
    chunks_per_visit = MOE_VISIT_ROWS // DISPATCH_ROWS
    chunk_valid = jnp.clip(jnp.repeat(vis_rows, chunks_per_visit)
                           - jnp.tile(jnp.arange(chunks_per_visit, dtype=jnp.int32) * DISPATCH_ROWS, max_visits),
                           0, DISPATCH_ROWS).astype(jnp.int32)

    xs = _dispatch(chunk_valid, src_rows, hn, n_rows)
    act = _gate_up(vis_expert, v_eff, vis_rows, xs, w_gate_up, b_gate_up)
    y_sorted = _down(vis_expert, v_eff, vis_rows, act, w_down, b_down)
    return _combine(pos_flat, h, meta_w, y_sorted)


def kernel(x_prompt, x_sample, cache_k, cache_v, cache_mem_k, cache_mem_v, page_table, mem_prompt, rel_bias, norm1_g, w_in, q_norm_g, k_norm_g, gmlp_ln_g, gmlp_ln_b, gmlp_ws, gmlp_bs, mem_norm_g, w_mem_kv, mq_norm_g, mk_norm_g, w_br_moba, w_br_gmlp, w_br_mem, w_out, norm2_g, w_router, b_router, w_gate_up, b_gate_up, w_down, b_down):
    batch, s_len, d = x_prompt.shape
    n_seq, dec_len, _ = x_sample.shape
    depth, n_phys, page, heads, hd = cache_k.shape
    _, _, mem, mem_heads, mem_hd = cache_mem_k.shape
    assert depth == 1 and batch == 1 and dec_len == 1
    moba_w, mem_w, gw = heads * hd, mem_heads * mem_hd, gmlp_ln_g.shape[-1]
    n_pages = page_table.shape[1]
    past_len = n_pages * page
    assert MOBA_BLOCK % page == 0 and past_len % MOBA_BLOCK == 0 and s_len % MOBA_BLOCK == 0
    ppb = MOBA_BLOCK // page
    nb = past_len // MOBA_BLOCK
    assert nb >= MOBA_TOPK
    n = s_len + n_seq

    x_all = jnp.concatenate([x_prompt.reshape(s_len, d), x_sample.reshape(n_seq, d)], axis=0)
    w_in2 = w_in.reshape(d, -1)
    xn = _rmsnorm(x_all, norm1_g, BF16)
    off = 0
    q = _proj(xn, w_in2, off, moba_w, "headnorm", q_norm_g, hd); off += moba_w
    k = _proj(xn, w_in2, off, moba_w, "headnorm", k_norm_g, hd); off += moba_w
    v = _proj(xn, w_in2, off, moba_w); off += moba_w
    z = _proj(xn, w_in2, off, 2 * gw, "gelu"); off += 2 * gw
    mq = _proj(xn, w_in2, off, mem_w, "headnorm", mq_norm_g, mem_hd); off += mem_w
    gates = _proj(xn, w_in2, off, 3 * d, "sigmoid")

    bias_tab = _bias_tables(rel_bias)
    att_p = _moba_prompt(q, k, v, bias_tab, s_len, heads, hd)
    q_s3, k_s3, v_s3 = (a[s_len:].reshape(n_seq, 1, moba_w) for a in (q, k, v))
    cache_k4 = cache_k.reshape(n_phys, page, heads, hd)
    cache_v4 = cache_v.reshape(n_phys, page, heads, hd)
    pt_flat = page_table.reshape(-1)
    sel = _sample_gate(pt_flat, q_s3.reshape(n_seq, heads, hd), cache_k4, n_seq, nb, ppb)
    sel_flat = jnp.swapaxes(sel[:, :MOBA_TOPK, :heads], 1, 2).reshape(-1)
    sample_bias = jnp.stack([bias_tab[:, 1, 0, :],
                             jnp.broadcast_to(bias_tab[:, 1, MOBA_BLOCK - 1, 0:1], (heads, MOBA_BLOCK)),
                             jnp.broadcast_to(bias_tab[:, 0, 0, 0:1], (heads, MOBA_BLOCK))], axis=1)
    att_s = _moba_sample(pt_flat, sel_flat, q_s3, k_s3, v_s3, sample_bias, cache_k4, cache_v4, n_seq, nb, ppb)
    att = jnp.concatenate([att_p, att_s.reshape(n_seq, moba_w)], axis=0)

    g_p, g_s, gv_s = _gmlp(z, s_len, gmlp_ln_g, gmlp_ln_b, gmlp_ws[0], gmlp_bs[0])
    g_out = jnp.concatenate([g_p, g_s], axis=0)

    mem_n = _rmsnorm(mem_prompt.reshape(mem, d), mem_norm_g, BF16)
    w_mem2 = w_mem_kv.reshape(d, 2 * mem_w)
    mk_p = _proj(mem_n, w_mem2, 0, mem_w, "headnorm", mk_norm_g, mem_hd)
    mv_p = _proj(mem_n, w_mem2, mem_w, mem_w)
    m_p = _mem_attn_prompt(mq, mk_p, mv_p, s_len, mem_heads)
    m_s = _mem_attn_sample(mq[s_len:].reshape(n_seq, 1, mem_w), cache_mem_k.reshape(n_seq, mem, mem_w),
                           cache_mem_v.reshape(n_seq, mem, mem_w), mem_heads)
    m_out = jnp.concatenate([m_p, m_s.reshape(n_seq, mem_w)], axis=0)

    mixed = _merge(att, g_out, m_out, gates, w_br_moba.reshape(moba_w, d), w_br_gmlp.reshape(gw, d), w_br_mem.reshape(mem_w, d))
    h = _proj(mixed, w_out.reshape(d, d), 0, d, "residual", x_all)
    n_experts = w_router.shape[-1]
    out = _moe(h, norm2_g, w_router.reshape(d, n_experts), b_router.reshape(n_experts),
               w_gate_up.reshape(n_experts, d, -1), b_gate_up.reshape(n_experts, -1),
               w_down.reshape(n_experts, -1, d), b_down.reshape(n_experts, d))

    y_prompt = out[:s_len].reshape(batch, s_len, d)
    y_sample = out[s_len:].reshape(n_seq, dec_len, d)
    k_prompt = k[:s_len].reshape(1, batch, s_len, heads, hd)
    v_prompt = v[:s_len].reshape(1, batch, s_len, heads, hd)
    mem_k_prompt = mk_p.reshape(1, batch, mem, mem_heads, mem_hd)
    mem_v_prompt = mv_p.reshape(1, batch, mem, mem_heads, mem_hd)
    k_sample = k[s_len:].reshape(1, n_seq, dec_len, heads, hd)
    v_sample = v[s_len:].reshape(1, n_seq, dec_len, heads, hd)
    gmlp_v_sample = gv_s.reshape(1, n_seq, dec_len, gmlp_ws.shape[1], -1)
    return (y_prompt, y_sample, k_prompt, v_prompt, mem_k_prompt, mem_v_prompt, k_sample, v_sample, gmlp_v_sample)
```

```python
import functools
import math

import numpy as np
import jax
import jax.numpy as jnp
from jax import lax
from jax.experimental import pallas as pl
from jax.experimental.pallas import tpu as pltpu

MOBA_BLOCK = 256
MOBA_TOPK = 3
REL_BUCKETS = 32
REL_MAX_DISTANCE = 128
TOP_K = 4
SWIGLU_LIMIT = 7.0
SWIGLU_ALPHA = 1.702
NORM_EPS = 1e-6

V7X_VMEM_BYTES = 64 * 1024 * 1024
VMEM_LIMIT = V7X_VMEM_BYTES - 12 * 1024 * 1024
LANES = 128

ROW_TILE = 1040
COL_TILE = 256
PROJ_COL_TILE = 512
NORM_ROWS = 256
MOBA_SUPER = 4
GATE_BLOCKS = 4
MOE_CHUNK = 512
MOE_TAIL = 128
MOE_COL_TILE = 512
MOE_VISIT_ROWS = 1280
DISPATCH_ROWS = 256
COMBINE_TOKENS = 64

F32 = jnp.float32
BF16 = jnp.bfloat16
HI = lax.Precision.HIGHEST
NT = (((1,), (1,)), ((), ()))
NEG_INF = float("-inf")
MASK_BIG = 2.0 ** 100


def _params(sem, vmem=VMEM_LIMIT):
    return pltpu.CompilerParams(dimension_semantics=sem, vmem_limit_bytes=vmem)


def _tile(n, pref):
    return n if n <= pref else pref


def _rmsnorm_kernel(x_ref, g_ref, o_ref):
    x = x_ref[...]
    y = x * lax.rsqrt(jnp.mean(x * x, axis=-1, keepdims=True) + NORM_EPS)
    o_ref[...] = (y * g_ref[...]).astype(o_ref.dtype)


def _rmsnorm(x, g, out_dtype):
    n, d = x.shape
    tm = _tile(n, NORM_ROWS)
    return pl.pallas_call(
        _rmsnorm_kernel,
        grid=(pl.cdiv(n, tm),),
        in_specs=[pl.BlockSpec((tm, d), lambda i: (i, 0)), pl.BlockSpec((1, d), lambda i: (0, 0))],
        out_specs=pl.BlockSpec((tm, d), lambda i: (i, 0)),
        out_shape=jax.ShapeDtypeStruct((n, d), out_dtype),
        compiler_params=_params(("parallel",)),
        name="rmsnorm",
    )(x, g.reshape(1, d))


def _proj_kernel(*refs, epilogue, head_dim):
    x_ref, w_ref, o_ref = refs[0], refs[1], refs[-1]
    acc = jnp.dot(x_ref[...], w_ref[...], preferred_element_type=F32)
    if epilogue == "headnorm":
        g = refs[2][...]
        for s in range(acc.shape[1] // head_dim):
            a = acc[:, s * head_dim:(s + 1) * head_dim]
            y = a * lax.rsqrt(jnp.mean(a * a, axis=-1, keepdims=True) + NORM_EPS)
            o_ref[:, s * head_dim:(s + 1) * head_dim] = (y * g).astype(o_ref.dtype)
    elif epilogue == "gelu":
        o_ref[...] = (acc * (lax.erf(acc * (1.0 / math.sqrt(2.0))) + 1.0) * 0.5).astype(o_ref.dtype)
    elif epilogue == "sigmoid":
        o_ref[...] = jax.nn.sigmoid(acc).astype(o_ref.dtype)
    elif epilogue == "residual":
        o_ref[...] = (refs[2][...] + acc).astype(o_ref.dtype)
    else:
        o_ref[...] = acc.astype(o_ref.dtype)


def _proj(x, w, col_off, width, epilogue="none", extra=None, head_dim=0, out_dtype=F32):
    n, k = x.shape
    tm = _tile(n, ROW_TILE)
    tn = _tile(width, PROJ_COL_TILE)
    while width % tn or col_off % tn:
        tn //= 2
    assert tn % LANES == 0
    joff = col_off // tn
    in_specs = [pl.BlockSpec((tm, k), lambda i, j: (i, 0)), pl.BlockSpec((k, tn), lambda i, j: (0, joff + j))]
    args = [x, w]
    if epilogue == "headnorm":
        assert tn % head_dim == 0
        in_specs.append(pl.BlockSpec((1, head_dim), lambda i, j: (0, 0)))
        args.append(extra.reshape(1, head_dim))
    elif epilogue == "residual":
        in_specs.append(pl.BlockSpec((tm, tn), lambda i, j: (i, j)))
        args.append(extra)
    return pl.pallas_call(
        functools.partial(_proj_kernel, epilogue=epilogue, head_dim=head_dim),
        grid=(pl.cdiv(n, tm), width // tn),
        in_specs=in_specs,
        out_specs=pl.BlockSpec((tm, tn), lambda i, j: (i, j)),
        out_shape=jax.ShapeDtypeStruct((n, width), out_dtype),
        compiler_params=_params(("parallel", "arbitrary")),
        name="proj_" + epilogue,
    )(*args)


def _t5_bucket_np(dist):
    n = np.maximum(dist, 0)
    max_exact = REL_BUCKETS // 2
    log_ratio = (np.log(np.maximum(n, 1).astype(np.float32) / np.float32(max_exact))
                 / np.float32(math.log(REL_MAX_DISTANCE / max_exact)))
    large = max_exact + (log_ratio * np.float32(REL_BUCKETS - max_exact)).astype(np.int32)
    return np.where(n < max_exact, n, np.minimum(large, REL_BUCKETS - 1)).astype(np.int32)


def _bias_table_kernel(rel_ref, bucket_ref, o_ref):
    h = pl.program_id(0)
    for t in range(2):
        bucket = bucket_ref[t]
        acc = jnp.zeros(bucket.shape, F32)
        for b in range(REL_BUCKETS):
            acc = jnp.where(bucket == b, rel_ref[b, h], acc)
        o_ref[0, t] = acc - rel_ref[REL_BUCKETS - 1, h]


def _bias_tables(rel_bias):
    assert MOBA_BLOCK >= REL_MAX_DISTANCE
    heads = rel_bias.shape[1]
    i = np.arange(MOBA_BLOCK)[:, None]
    j = np.arange(MOBA_BLOCK)[None, :]
    buckets = jnp.asarray(np.stack([_t5_bucket_np(i - j), _t5_bucket_np(MOBA_BLOCK + i - j)]))
    return pl.pallas_call(
        _bias_table_kernel,
        grid=(heads,),
        in_specs=[pl.BlockSpec(memory_space=pltpu.SMEM),
                  pl.BlockSpec((2, MOBA_BLOCK, MOBA_BLOCK), lambda h: (0, 0, 0))],
        out_specs=pl.BlockSpec((1, 2, MOBA_BLOCK, MOBA_BLOCK), lambda h: (h, 0, 0, 0)),
        out_shape=jax.ShapeDtypeStruct((heads, 2, MOBA_BLOCK, MOBA_BLOCK), F32),
        compiler_params=_params(("arbitrary",)),
        name="bias_tables",
    )(rel_bias, buckets)


def _top_blocks(gate, blk_iota, n_blocks, axis):
    picked = []
    mask = jnp.zeros(gate.shape, jnp.bool_)
    g = gate
    for _ in range(MOBA_TOPK):
        m = jnp.max(g, axis=axis, keepdims=True)
        first = jnp.min(jnp.where((g == m) & (m > NEG_INF), blk_iota, n_blocks), axis=axis, keepdims=True)
        pick = blk_iota == first
        picked.append(first)
        mask = mask | pick
        g = jnp.where(pick, NEG_INF, g)
    return picked, mask


def _moba_prompt_kernel(q_ref, k_ref, v_ref, bias_ref, o_ref, kmean_ref, kaug_ref, vbf_ref, s_ref, *, scale):
    h, qb = pl.program_id(0), pl.program_id(1)
    blk = MOBA_BLOCK
    s_len, hd = k_ref.shape
    nb = s_len // blk

    @pl.when((h == 0) & (qb == 0))
    def _():
        key_blk = lax.broadcasted_iota(jnp.int32, (s_len, LANES), 0) // blk
        lane = lax.broadcasted_iota(jnp.int32, (s_len, LANES), 1)
        kaug_ref[:, hd:] = jnp.where(key_blk == lane, -MASK_BIG, 0.0).astype(BF16)

    @pl.when(qb == 0)
    def _():
        kmean_ref[...] = jnp.zeros(kmean_ref.shape, F32)
        kmean_ref[:nb, :] = jnp.mean(k_ref[...].reshape(nb, blk, hd), axis=1)
        kaug_ref[:, :hd] = k_ref[...].astype(BF16)
        vbf_ref[...] = v_ref[...].astype(BF16)

    q = q_ref[...]
    gate = lax.dot_general(q, kmean_ref[...], NT, precision=HI, preferred_element_type=F32)
    blk_iota = lax.broadcasted_iota(jnp.int32, (blk, LANES), 1)
    gate = jnp.where(blk_iota < qb, gate, NEG_INF)
    _, sel = _top_blocks(gate, blk_iota, LANES, axis=1)
    sel_f = sel.astype(F32)

    qs = (q * scale).astype(BF16)
    row = lax.broadcasted_iota(jnp.int32, (blk, blk), 0)
    col = lax.broadcasted_iota(jnp.int32, (blk, blk), 1)

    def logits(start, rows):
        return lax.dot_general(qs, kaug_ref[pl.ds(start, rows), :hd], NT, preferred_element_type=F32)

    def pv(p, start, rows):
        return jnp.dot(p.astype(BF16), vbf_ref[pl.ds(start, rows), :], preferred_element_type=F32)

    def chose(kb):
        return jnp.max(jnp.where(blk_iota == kb, sel_f, 0.0), axis=1, keepdims=True) > 0.0

    def update(carry, s, start, rows):
        m, l, acc = carry
        m_new = jnp.maximum(m, jnp.max(s, axis=1, keepdims=True))
        alpha = jnp.exp(m - m_new)
        p = jnp.exp(s - m_new)
        return m_new, alpha * l + jnp.sum(p, axis=1, keepdims=True), alpha * acc + pv(p, start, rows)

    first = qb == 0
    prev_blk = jnp.maximum(qb - 1, 0)
    near = pl.multiple_of(prev_blk * blk, blk)
    raw = logits(near, 2 * blk)
    causal = jnp.where(row >= col, 0.0, NEG_INF)
    picked_prev = jnp.where(chose(prev_blk), 0.0, NEG_INF)
    s_lo = raw[:, :blk] + jnp.where(first, bias_ref[0, 0] + causal, bias_ref[0, 1] + picked_prev)
    s_hi = raw[:, blk:] + jnp.where(first, NEG_INF, bias_ref[0, 0] + causal)
    s = jnp.concatenate([s_lo, s_hi], axis=1)
    m = jnp.max(s, axis=1, keepdims=True)
    p = jnp.exp(s - m)
    carry = (m, jnp.sum(p, axis=1, keepdims=True), pv(p, near, 2 * blk))

    blocked = jnp.where(sel & (blk_iota < qb - 1), 0.0, 1.0)
    q_aug = jnp.concatenate([qs, blocked.astype(BF16)], axis=1)
    span = MOBA_SUPER * blk

    def masked_logits(g):
        start = pl.multiple_of(g * span, span)
        return lax.dot_general(q_aug, kaug_ref[pl.ds(start, span), :], NT, preferred_element_type=F32)

    def body(g, carry):
        nxt = masked_logits(g + 1)
        carry = update(carry, s_ref[...], pl.multiple_of(g * span, span), span)
        s_ref[...] = nxt
        return carry

    last = jnp.maximum((jnp.maximum(qb - 1, 0) + MOBA_SUPER - 1) // MOBA_SUPER - 1, 0)
    s_ref[...] = masked_logits(0)
    carry = lax.fori_loop(0, last, body, carry)
    m, l, acc = update(carry, s_ref[...], pl.multiple_of(last * span, span), span)
    o_ref[...] = (acc / l).astype(o_ref.dtype)


def _moba_prompt(q, k, v, bias_tab, s_len, heads, hd):
    assert s_len % (MOBA_SUPER * MOBA_BLOCK) == 0 and s_len // MOBA_BLOCK <= LANES
    nqb = s_len // MOBA_BLOCK
    return pl.pallas_call(
        functools.partial(_moba_prompt_kernel, scale=hd ** -0.5),
        grid=(heads, nqb),
        in_specs=[pl.BlockSpec((MOBA_BLOCK, hd), lambda h, i: (i, h)),
                  pl.BlockSpec((s_len, hd), lambda h, i: (0, h)),
                  pl.BlockSpec((s_len, hd), lambda h, i: (0, h)),
                  pl.BlockSpec((1, 2, MOBA_BLOCK, MOBA_BLOCK), lambda h, i: (h, 0, 0, 0))],
        out_specs=pl.BlockSpec((MOBA_BLOCK, hd), lambda h, i: (i, h)),
        out_shape=jax.ShapeDtypeStruct((s_len, heads * hd), BF16),
        scratch_shapes=[pltpu.VMEM((LANES, hd), F32), pltpu.VMEM((s_len, hd + LANES), BF16),
                        pltpu.VMEM((s_len, hd), BF16), pltpu.VMEM((MOBA_BLOCK, MOBA_SUPER * MOBA_BLOCK), F32)],
        compiler_params=_params(("arbitrary", "arbitrary")),
        name="moba_prompt",
    )(q, k, v, bias_tab)


def _sample_gate_kernel(pt_ref, q_ref, *refs, ppb, bps):
    page_refs, sel_ref, kmean_ref = refs[:bps * ppb], refs[bps * ppb], refs[bps * ppb + 1]
    j = pl.program_id(1)
    nb, heads, _ = kmean_ref.shape
    for t in range(bps):
        total = jnp.sum(page_refs[t * ppb][0], axis=0)
        for p in range(1, ppb):
            total = total + jnp.sum(page_refs[t * ppb + p][0], axis=0)
        kmean_ref[j * bps + t] = total * (1.0 / MOBA_BLOCK)

    @pl.when(j == pl.num_programs(1) - 1)
    def _():
        gate = jnp.sum(kmean_ref[...] * q_ref[...], axis=-1)
        blk_iota = lax.broadcasted_iota(jnp.int32, (nb, heads), 0)
        picked, _ = _top_blocks(gate, blk_iota, nb, axis=0)
        sel_ref[...] = jnp.zeros(sel_ref.shape, jnp.int32)
        for t in range(MOBA_TOPK):
            sel_ref[0, t:t + 1, :heads] = picked[t]


def _sample_gate(page_table_flat, q_s, cache_k, n_seq, nb, ppb):
    _, page, heads, hd = cache_k.shape
    bps = _tile(nb, GATE_BLOCKS)
    assert nb % bps == 0 and heads <= LANES and MOBA_TOPK <= 8
    n_pages = nb * ppb
    page_specs = [pl.BlockSpec((1, page, heads, hd),
                               lambda b, j, pt, i=i: (pt[b * n_pages + j * bps * ppb + i], 0, 0, 0))
                  for i in range(bps * ppb)]
    return pl.pallas_call(
        functools.partial(_sample_gate_kernel, ppb=ppb, bps=bps),
        grid_spec=pltpu.PrefetchScalarGridSpec(
            num_scalar_prefetch=1,
            grid=(n_seq, nb // bps),
            in_specs=[pl.BlockSpec((1, heads, hd), lambda b, j, pt: (b, 0, 0))] + page_specs,
            out_specs=pl.BlockSpec((1, 8, LANES), lambda b, j, pt: (b, 0, 0)),
            scratch_shapes=[pltpu.VMEM((nb, heads, hd), F32)]),
        out_shape=jax.ShapeDtypeStruct((n_seq, 8, LANES), jnp.int32),
        compiler_params=_params(("parallel", "arbitrary")),
        name="moba_sample_gate",
    )(page_table_flat, q_s, *([cache_k] * (bps * ppb)))


def _moba_sample_kernel(pt_ref, sel_ref, q_ref, kn_ref, vn_ref, sb_ref, ck_ref, cv_ref, o_ref, kbuf, vbuf, sems,
                        *, nb, ppb, scale):
    b = pl.program_id(0)
    _, heads, n_keys, hd = kbuf.shape
    page = n_keys // (MOBA_TOPK * ppb)
    n_pages = nb * ppb
    pages = [(h, t, p) for h in range(heads) for t in range(MOBA_TOPK) for p in range(ppb)]

    def page_copies(seq, h, t, p):
        half = seq % 2
        phys = pt_ref[seq * n_pages + sel_ref[(seq * heads + h) * MOBA_TOPK + t] * ppb + p]
        dst = pl.ds((t * ppb + p) * page, page)
        return (pltpu.make_async_copy(ck_ref.at[phys, :, h, :], kbuf.at[half, h, dst, :], sems.at[half, 0]),
                pltpu.make_async_copy(cv_ref.at[phys, :, h, :], vbuf.at[half, h, dst, :], sems.at[half, 1]))

    def fetch(seq):
        for page_id in pages:
            for copy in page_copies(seq, *page_id):
                copy.start()

    @pl.when(b == 0)
    def _():
        fetch(b)

    @pl.when(b + 1 < pl.num_programs(0))
    def _():
        fetch(b + 1)

    for page_id in pages:
        for copy in page_copies(b, *page_id):
            copy.wait()

    cur = b % 2
    for h in range(heads):
        cols = slice(h * hd, (h + 1) * hd)
        q = q_ref[0, :, cols]
        q8 = jnp.broadcast_to(q, (8, hd)).astype(BF16)
        near, far, self_bias = sb_ref[h, 0:1, :], sb_ref[h, 1:2, :], sb_ref[h, 2:3, 0:1]
        bias = jnp.concatenate(
            [jnp.where(sel_ref[(b * heads + h) * MOBA_TOPK + t] == nb - 1, near, far) for t in range(MOBA_TOPK)], axis=1)
        s = lax.dot_general(q8, kbuf[cur, h].astype(BF16), NT, preferred_element_type=F32)[0:1] * scale + bias
        s_self = jnp.sum(q * kn_ref[0, :, cols], axis=1, keepdims=True) * scale + self_bias
        m = jnp.maximum(jnp.max(s, axis=1, keepdims=True), s_self)
        p = jnp.exp(s - m)
        p_self = jnp.exp(s_self - m)
        l = jnp.sum(p, axis=1, keepdims=True) + p_self
        p8 = jnp.broadcast_to(p, (8, n_keys)).astype(BF16)
        acc = jnp.dot(p8, vbuf[cur, h].astype(BF16), preferred_element_type=F32)[0:1] + p_self * vn_ref[0, :, cols]
        o_ref[0, :, cols] = (acc / l).astype(o_ref.dtype)


def _moba_sample(page_table_flat, sel_flat, q_s3, k_s3, v_s3, sample_bias, cache_k, cache_v, n_seq, nb, ppb):
    _, page, heads, hd = cache_k.shape
    n_keys = MOBA_TOPK * ppb * page
    row_spec = pl.BlockSpec((1, 1, heads * hd), lambda b, pt, sel: (b, 0, 0))
    return pl.pallas_call(
        functools.partial(_moba_sample_kernel, nb=nb, ppb=ppb, scale=hd ** -0.5),
        grid_spec=pltpu.PrefetchScalarGridSpec(
            num_scalar_prefetch=2,
            grid=(n_seq,),
            in_specs=[row_spec, row_spec, row_spec,
                      pl.BlockSpec((heads, 3, MOBA_BLOCK), lambda b, pt, sel: (0, 0, 0)),
                      pl.BlockSpec(memory_space=pl.ANY), pl.BlockSpec(memory_space=pl.ANY)],
            out_specs=row_spec,
            scratch_shapes=[pltpu.VMEM((2, heads, n_keys, hd), F32), pltpu.VMEM((2, heads, n_keys, hd), F32),
                            pltpu.SemaphoreType.DMA((2, 2))]),
        out_shape=jax.ShapeDtypeStruct((n_seq, 1, heads * hd), BF16),
        compiler_params=_params(("arbitrary",)),
        name="moba_sample",
    )(page_table_flat, sel_flat, q_s3, k_s3, v_s3, sample_bias, cache_k, cache_v)


def _layernorm(x, g, b):
    mu = jnp.mean(x, axis=-1, keepdims=True)
    var = jnp.mean(jnp.square(x - mu), axis=-1, keepdims=True)
    return (x - mu) * lax.rsqrt(var + NORM_EPS) * g + b


def _gmlp_prompt_kernel(z_ref, g_ref, b_ref, ws_ref, bs_ref, o_ref):
    groups, chunk, _ = ws_ref.shape
    gw = g_ref.shape[1]
    gd = gw // groups
    u = z_ref[:, :gw]
    gv = _layernorm(z_ref[:, gw:], g_ref[...], b_ref[...]).astype(BF16)
    row = lax.broadcasted_iota(jnp.int32, (chunk, chunk), 0)
    col = lax.broadcasted_iota(jnp.int32, (chunk, chunk), 1)
    for g in range(groups):
        ws = jnp.where(row >= col, ws_ref[g], 0.0).astype(BF16)
        mixed = jnp.dot(ws, gv[:, g * gd:(g + 1) * gd], preferred_element_type=F32) + bs_ref[:, g:g + 1]
        o_ref[:, g * gd:(g + 1) * gd] = (u[:, g * gd:(g + 1) * gd] * mixed).astype(o_ref.dtype)


def _gmlp_sample_kernel(z_ref, g_ref, b_ref, w0_ref, b0_ref, o_ref, gv_ref):
    gw = g_ref.shape[1]
    gv = _layernorm(z_ref[:, gw:], g_ref[...], b_ref[...])
    gv_ref[...] = gv
    o_ref[...] = (z_ref[:, :gw] * (w0_ref[...] * gv + b0_ref[...])).astype(o_ref.dtype)


def _gmlp(z, s_len, ln_g, ln_b, ws, bs):
    n, two_gw = z.shape
    gw = two_gw // 2
    groups, chunk, _ = ws.shape
    gd = gw // groups
    assert s_len % chunk == 0
    g2, b2 = ln_g.reshape(1, gw), ln_b.reshape(1, gw)
    out_p = pl.pallas_call(
        _gmlp_prompt_kernel,
        grid=(s_len // chunk,),
        in_specs=[pl.BlockSpec((chunk, two_gw), lambda c: (c, 0)),
                  pl.BlockSpec((1, gw), lambda c: (0, 0)), pl.BlockSpec((1, gw), lambda c: (0, 0)),
                  pl.BlockSpec((groups, chunk, chunk), lambda c: (0, 0, 0)),
                  pl.BlockSpec((chunk, groups), lambda c: (0, 0))],
        out_specs=pl.BlockSpec((chunk, gw), lambda c: (c, 0)),
        out_shape=jax.ShapeDtypeStruct((s_len, gw), BF16),
        compiler_params=_params(("parallel",)),
        name="gmlp_prompt",
    )(z, g2, b2, ws, bs.T)
    z_s = z[s_len:]
    n_s = n - s_len
    w0 = jnp.repeat(ws[:, 0, 0], gd).reshape(1, gw)
    b0 = jnp.repeat(bs[:, 0], gd).reshape(1, gw)
    out_s, gv_s = pl.pallas_call(
        _gmlp_sample_kernel,
        out_shape=(jax.ShapeDtypeStruct((n_s, gw), BF16), jax.ShapeDtypeStruct((n_s, gw), F32)),
        name="gmlp_sample",
    )(z_s, g2, b2, w0, b0)
    return out_p, out_s, gv_s


def _mem_attn_kernel(q_ref, k_ref, v_ref, o_ref, *, heads, scale, batched):
    q = q_ref[0] if batched else q_ref[...]
    k = k_ref[0] if batched else k_ref[...]
    v = v_ref[0] if batched else v_ref[...]
    rows = max(q.shape[0], 8)
    hd = q.shape[1] // heads
    for h in range(heads):
        sl = slice(h * hd, (h + 1) * hd)
        qh = jnp.broadcast_to(q[:, sl], (rows, hd)).astype(BF16)
        s = lax.dot_general(qh, k[:, sl].astype(BF16), NT, preferred_element_type=F32) * scale
        p = jnp.exp(s - jnp.max(s, axis=1, keepdims=True))
        l = jnp.sum(p, axis=1, keepdims=True)
        out = jnp.dot(p.astype(BF16), v[:, sl].astype(BF16), preferred_element_type=F32) / l
        out = out[:q.shape[0]].astype(o_ref.dtype)
        if batched:
            o_ref[0, :, sl] = out
        else:
            o_ref[:, sl] = out


def _mem_attn_prompt(mq, mk, mv, s_len, heads):
    width = mq.shape[1]
    mem = mk.shape[0]
    tq = _tile(s_len, 512)
    return pl.pallas_call(
        functools.partial(_mem_attn_kernel, heads=heads, scale=(width // heads) ** -0.5, batched=False),
        grid=(s_len // tq,),
        in_specs=[pl.BlockSpec((tq, width), lambda i: (i, 0)),
                  pl.BlockSpec((mem, width), lambda i: (0, 0)), pl.BlockSpec((mem, width), lambda i: (0, 0))],
        out_specs=pl.BlockSpec((tq, width), lambda i: (i, 0)),
        out_shape=jax.ShapeDtypeStruct((s_len, width), BF16),
        compiler_params=_params(("parallel",)),
        name="mem_attn_prompt",
    )(mq, mk, mv)


def _mem_attn_sample(mq_s3, ck, cv, heads):
    n_seq, mem, width = ck.shape
    return pl.pallas_call(
        functools.partial(_mem_attn_kernel, heads=heads, scale=(width // heads) ** -0.5, batched=True),
        grid=(n_seq,),
        in_specs=[pl.BlockSpec((1, 1, width), lambda b: (b, 0, 0)),
                  pl.BlockSpec((1, mem, width), lambda b: (b, 0, 0)), pl.BlockSpec((1, mem, width), lambda b: (b, 0, 0))],
        out_specs=pl.BlockSpec((1, 1, width), lambda b: (b, 0, 0)),
        out_shape=jax.ShapeDtypeStruct((n_seq, 1, width), BF16),
        compiler_params=_params(("parallel",)),
        name="mem_attn_sample",
    )(mq_s3, ck, cv)


def _merge_kernel(a_ref, g_ref, m_ref, ga_ref, gg_ref, gm_ref, wa_ref, wg_ref, wm_ref, o_ref):
    mixed = ga_ref[...] * jnp.dot(a_ref[...], wa_ref[...], preferred_element_type=F32)
    mixed = mixed + gg_ref[...] * jnp.dot(g_ref[...], wg_ref[...], preferred_element_type=F32)
    mixed = mixed + gm_ref[...] * jnp.dot(m_ref[...], wm_ref[...], preferred_element_type=F32)
    o_ref[...] = mixed.astype(o_ref.dtype)


def _merge(att, gout, mout, gates, w_a, w_g, w_m):
    n = att.shape[0]
    d = w_a.shape[1]
    tm = _tile(n, ROW_TILE)
    tn = _tile(d, COL_TILE)
    nj = d // tn

    def act_spec(a):
        return pl.BlockSpec((tm, a.shape[1]), lambda i, j: (i, 0))

    def gate_spec(branch):
        return pl.BlockSpec((tm, tn), lambda i, j: (i, branch * nj + j))

    def w_spec(w):
        return pl.BlockSpec((w.shape[0], tn), lambda i, j: (0, j))

    return pl.pallas_call(
        _merge_kernel,
        grid=(pl.cdiv(n, tm), nj),
        in_specs=[act_spec(att), act_spec(gout), act_spec(mout), gate_spec(0), gate_spec(1), gate_spec(2),
                  w_spec(w_a), w_spec(w_g), w_spec(w_m)],
        out_specs=pl.BlockSpec((tm, tn), lambda i, j: (i, j)),
        out_shape=jax.ShapeDtypeStruct((n, d), BF16),
        compiler_params=_params(("parallel", "arbitrary")),
        name="merge",
    )(att, gout, mout, gates, gates, gates, w_a, w_g, w_m)


def _router_kernel(h_ref, g_ref, wr_ref, br_ref, hn_ref, mi_ref, mw_ref, cnt_ref, carry_ref, *, n_tokens, n_experts):
    i = pl.program_id(0)
    tm = h_ref.shape[0]

    @pl.when(i == 0)
    def _():
        carry_ref[...] = jnp.zeros(carry_ref.shape, F32)

    x = h_ref[...]
    hn = x * lax.rsqrt(jnp.mean(x * x, axis=-1, keepdims=True) + NORM_EPS) * g_ref[...]
    hn_ref[...] = hn
    logits = jnp.dot(hn, wr_ref[...], precision=HI, preferred_element_type=F32) + br_ref[...]
    e_iota = lax.broadcasted_iota(jnp.int32, (tm, n_experts), 1)
    vals, ids = [], []
    l = logits
    for _ in range(TOP_K):
        m = jnp.max(l, axis=1, keepdims=True)
        first = jnp.min(jnp.where(l == m, e_iota, n_experts), axis=1, keepdims=True)
        vals.append(m)
        ids.append(first)
        l = jnp.where(e_iota == first, NEG_INF, l)
    ex = [jnp.exp(v - vals[0]) for v in vals]
    denom = ex[0]
    for e in ex[1:]:
        denom = denom + e
    valid = (i * tm + lax.broadcasted_iota(jnp.int32, (tm, 1), 0)) < n_tokens
    multi_hot = jnp.zeros((tm, n_experts), F32)
    for k in range(TOP_K):
        multi_hot = multi_hot + (e_iota == ids[k]).astype(F32)
    multi_hot = jnp.where(valid, multi_hot, 0.0)
    lower = (lax.broadcasted_iota(jnp.int32, (tm, tm), 0) > lax.broadcasted_iota(jnp.int32, (tm, tm), 1)).astype(BF16)
    before = jnp.dot(lower, multi_hot.astype(BF16), preferred_element_type=F32) + carry_ref[:, :n_experts]
    lane = lax.broadcasted_iota(jnp.int32, (tm, LANES), 1)
    meta_i = jnp.zeros((tm, LANES), jnp.int32)
    meta_w = jnp.zeros((tm, LANES), F32)
    for k in range(TOP_K):
        rank = jnp.sum(jnp.where(e_iota == ids[k], before, 0.0), axis=1, keepdims=True).astype(jnp.int32)
        meta_i = jnp.where(lane == k, ids[k], meta_i)
        meta_i = jnp.where(lane == TOP_K + k, rank, meta_i)
        meta_w = jnp.where(lane == k, ex[k] / denom, meta_w)
    mi_ref[...] = meta_i
    mw_ref[...] = meta_w
    carry_ref[:, :n_experts] = carry_ref[:, :n_experts] + jnp.sum(multi_hot, axis=0, keepdims=True)
    cnt_ref[...] = carry_ref[...].astype(jnp.int32)


def _router(h, g, w_router, b_router):
    n, d = h.shape
    n_experts = w_router.shape[1]
    assert n_experts <= LANES
    tm = _tile(n, NORM_ROWS)
    return pl.pallas_call(
        functools.partial(_router_kernel, n_tokens=n, n_experts=n_experts),
        grid=(pl.cdiv(n, tm),),
        in_specs=[pl.BlockSpec((tm, d), lambda i: (i, 0)), pl.BlockSpec((1, d), lambda i: (0, 0)),
                  pl.BlockSpec((d, n_experts), lambda i: (0, 0)), pl.BlockSpec((1, n_experts), lambda i: (0, 0))],
        out_specs=[pl.BlockSpec((tm, d), lambda i: (i, 0)), pl.BlockSpec((tm, LANES), lambda i: (i, 0)),
                   pl.BlockSpec((tm, LANES), lambda i: (i, 0)), pl.BlockSpec((1, LANES), lambda i: (0, 0))],
        out_shape=(jax.ShapeDtypeStruct((n, d), F32), jax.ShapeDtypeStruct((n, LANES), jnp.int32),
                   jax.ShapeDtypeStruct((n, LANES), F32), jax.ShapeDtypeStruct((1, LANES), jnp.int32)),
        scratch_shapes=[pltpu.VMEM((1, LANES), F32)],
        compiler_params=_params(("arbitrary",)),
        name="router",
    )(h, g.reshape(1, d), w_router, b_router.reshape(1, n_experts))


def _dispatch_kernel(valid_ref, src_ref, hn_ref, o_ref, buf_ref, sem):
    c = pl.program_id(0)
    rows = buf_ref.shape[0]

    def row_copy(r, src_row):
        return pltpu.make_async_copy(hn_ref.at[pl.ds(src_row, 1)], buf_ref.at[pl.ds(r, 1)], sem)

    @pl.when(c == 0)
    def _():
        buf_ref[...] = jnp.zeros(buf_ref.shape, buf_ref.dtype)

    @pl.when(valid_ref[c] > 0)
    def _():
        def start(r, carry):
            row_copy(r, src_ref[r]).start()
            return carry

        def wait(r, carry):
            row_copy(r, 0).wait()
            return carry

        lax.fori_loop(0, valid_ref[c], start, 0)
        lax.fori_loop(0, valid_ref[c], wait, 0)
        o_ref[...] = buf_ref[...].astype(o_ref.dtype)

    @pl.when(valid_ref[c] == 0)
    def _():
        o_ref[...] = jnp.zeros(o_ref.shape, o_ref.dtype)


def _dispatch(chunk_valid, src_rows, hn, n_rows):
    d = hn.shape[1]
    return pl.pallas_call(
        _dispatch_kernel,
        grid_spec=pltpu.PrefetchScalarGridSpec(
            num_scalar_prefetch=1,
            grid=(n_rows // DISPATCH_ROWS,),
            in_specs=[pl.BlockSpec((DISPATCH_ROWS,), lambda c, valid: (c,), memory_space=pltpu.SMEM),
                      pl.BlockSpec(memory_space=pl.ANY)],
            out_specs=pl.BlockSpec((DISPATCH_ROWS, d), lambda c, valid: (c, 0)),
            scratch_shapes=[pltpu.VMEM((DISPATCH_ROWS, d), F32), pltpu.SemaphoreType.DMA(())]),
        out_shape=jax.ShapeDtypeStruct((n_rows, d), BF16),
        compiler_params=_params(("arbitrary",)),
        name="moe_dispatch",
    )(chunk_valid, src_rows, hn)


def _visit_specs(n_col_tiles):
    def col(v, j, vr):
        return jnp.where(vr[v] > 0, j, n_col_tiles - 1)

    def rows(v, j, ve, vb, vr):
        return (vb[v], 0)

    def weight(v, j, ve, vb, vr):
        return (ve[v], 0, col(v, j, vr))

    def out(v, j, ve, vb, vr):
        return (v, j)

    return rows, weight, out


def _run_visit(n_rows, o_ref, compute, finish, stage_ref):
    units_per_chunk = MOE_CHUNK // MOE_TAIL
    full = n_rows // MOE_CHUNK
    tail_units = (n_rows - full * MOE_CHUNK + MOE_TAIL - 1) // MOE_TAIL
    merge = (full > 0) & (tail_units == 1)
    full = jnp.where(merge, full - 1, full)
    tail_units = jnp.where(merge, units_per_chunk + 1, tail_units)

    @pl.when((pl.program_id(0) == 0) & (pl.program_id(1) == 0))
    def _():
        stage_ref[...] = jnp.zeros(stage_ref.shape, stage_ref.dtype)

    @pl.when(full > 0)
    def _():
        stage_ref[...] = compute(0, MOE_CHUNK)

        def body(c, carry):
            r0 = pl.multiple_of(c * MOE_CHUNK, MOE_CHUNK)
            nxt = compute(r0, MOE_CHUNK)
            finish(stage_ref[...], pl.multiple_of(r0 - MOE_CHUNK, MOE_CHUNK), MOE_CHUNK)
            stage_ref[...] = nxt
            return carry

        lax.fori_loop(1, full, body, 0)

    staged_r = pl.multiple_of(jnp.maximum(full - 1, 0) * MOE_CHUNK, MOE_CHUNK)
    r_full = pl.multiple_of(full * MOE_CHUNK, MOE_CHUNK)
    for units in range(1, units_per_chunk + 2):
        @pl.when(tail_units == units)
        def _(rows=units * MOE_TAIL):
            tail = compute(r_full, rows)
            finish(stage_ref[...], staged_r, MOE_CHUNK)
            finish(tail, r_full, rows)

    @pl.when((tail_units == 0) & (full > 0))
    def _():
        finish(stage_ref[...], staged_r, MOE_CHUNK)

    def zero(c, carry):
        o_ref[pl.ds(pl.multiple_of(c * MOE_TAIL, MOE_TAIL), MOE_TAIL), :] = jnp.zeros((MOE_TAIL, o_ref.shape[1]), o_ref.dtype)
        return carry

    lax.fori_loop(full * units_per_chunk + tail_units, o_ref.shape[0] // MOE_TAIL, zero, 0)


def _gate_up_kernel(ve_ref, vb_ref, vr_ref, x_ref, w_ref, b_ref, p_ref, o_ref, wb_ref, stage_ref):
    @pl.when(vr_ref[pl.program_id(0)] > 0)
    def _():
        wb_ref[...] = w_ref[0].astype(BF16)

    tn = wb_ref.shape[1]

    def compute(r0, rows):
        return jnp.dot(x_ref[pl.ds(r0, rows), :], wb_ref[...], preferred_element_type=F32) + b_ref[0]

    def finish(gu, r0, rows):
        nxt = pltpu.roll(gu, tn - 1, 1)
        glu = jnp.minimum(gu, SWIGLU_LIMIT)
        lin = jnp.clip(nxt, -SWIGLU_LIMIT, SWIGLU_LIMIT)
        even = (lax.broadcasted_iota(jnp.int32, (rows, tn), 1) % 2) == 0
        act = jnp.where(even, glu * jax.nn.sigmoid(SWIGLU_ALPHA * glu) * (lin + 1.0), 0.0)
        o_ref[pl.ds(r0, rows), :] = jnp.dot(act.astype(BF16), p_ref[...], preferred_element_type=F32).astype(o_ref.dtype)

    _run_visit(vr_ref[pl.program_id(0)], o_ref, compute, finish, stage_ref)


def _gate_up(vis_expert, vis_block, vis_rows, xs, w_gate_up, b_gate_up):
    n_rows, d = xs.shape
    n_experts, _, two_f = w_gate_up.shape
    tn = _tile(two_f, MOE_COL_TILE)
    sel = np.zeros((tn, tn // 2), np.float32)
    sel[2 * np.arange(tn // 2), np.arange(tn // 2)] = 1.0
    rows, weight, out = _visit_specs(two_f // tn)
    return pl.pallas_call(
        _gate_up_kernel,
        grid_spec=pltpu.PrefetchScalarGridSpec(
            num_scalar_prefetch=3,
            grid=(n_rows // MOE_VISIT_ROWS, two_f // tn),
            in_specs=[pl.BlockSpec((MOE_VISIT_ROWS, d), rows),
                      pl.BlockSpec((1, d, tn), weight),
                      pl.BlockSpec((1, 1, tn), weight),
                      pl.BlockSpec((tn, tn // 2), lambda v, j, ve, vb, vr: (0, 0))],
            out_specs=pl.BlockSpec((MOE_VISIT_ROWS, tn // 2), out),
            scratch_shapes=[pltpu.VMEM((d, tn), BF16), pltpu.VMEM((MOE_CHUNK, tn), F32)]),
        out_shape=jax.ShapeDtypeStruct((n_rows, two_f // 2), BF16),
        compiler_params=_params(("arbitrary", "arbitrary")),
        name="moe_gate_up",
    )(vis_expert, vis_block, vis_rows, xs, w_gate_up, b_gate_up.reshape(n_experts, 1, two_f), jnp.asarray(sel, BF16))


def _down_kernel(ve_ref, vb_ref, vr_ref, x_ref, w_ref, b_ref, o_ref, wb_ref, stage_ref):
    @pl.when(vr_ref[pl.program_id(0)] > 0)
    def _():
        wb_ref[...] = w_ref[0].astype(BF16)

    def compute(r0, rows):
        return jnp.dot(x_ref[pl.ds(r0, rows), :], wb_ref[...], preferred_element_type=F32) + b_ref[0]

    def finish(y, r0, rows):
        o_ref[pl.ds(r0, rows), :] = y

    _run_visit(vr_ref[pl.program_id(0)], o_ref, compute, finish, stage_ref)


def _down(vis_expert, vis_block, vis_rows, act, w_down, b_down):
    n_rows, f = act.shape
    n_experts, _, d = w_down.shape
    tn = _tile(d, MOE_COL_TILE)
    rows, weight, out = _visit_specs(d // tn)
    return pl.pallas_call(
        _down_kernel,
        grid_spec=pltpu.PrefetchScalarGridSpec(
            num_scalar_prefetch=3,
            grid=(n_rows // MOE_VISIT_ROWS, d // tn),
            in_specs=[pl.BlockSpec((MOE_VISIT_ROWS, f), rows),
                      pl.BlockSpec((1, f, tn), weight),
                      pl.BlockSpec((1, 1, tn), weight)],
            out_specs=pl.BlockSpec((MOE_VISIT_ROWS, tn), out),
            scratch_shapes=[pltpu.VMEM((f, tn), BF16), pltpu.VMEM((MOE_CHUNK, tn), F32)]),
        out_shape=jax.ShapeDtypeStruct((n_rows, d), F32),
        compiler_params=_params(("arbitrary", "arbitrary")),
        name="moe_down",
    )(vis_expert, vis_block, vis_rows, act, w_down, b_down.reshape(n_experts, 1, d))


def _combine_kernel(pos_ref, h_ref, w_ref, y_ref, op_ref, os_ref, buf_ref, sem):
    tt = h_ref.shape[0]

    def row_copy(t, k, src_row):
        return pltpu.make_async_copy(y_ref.at[pl.ds(src_row, 1)], buf_ref.at[pl.ds(k * tt + t, 1)], sem)

    def start(t, carry):
        for k in range(TOP_K):
            row_copy(t, k, pos_ref[t * TOP_K + k]).start()
        return carry

    def wait(t, carry):
        for k in range(TOP_K):
            row_copy(t, k, 0).wait()
        return carry

    lax.fori_loop(0, tt, start, 0)
    lax.fori_loop(0, tt, wait, 0)
    out = h_ref[...]
    for k in range(TOP_K):
        out = out + w_ref[:, k:k + 1] * buf_ref[k * tt:(k + 1) * tt, :]
    is_prompt = pl.program_id(0) < pl.num_programs(0) - 1

    @pl.when(is_prompt)
    def _():
        op_ref[...] = out

    @pl.when(jnp.logical_not(is_prompt))
    def _():
        os_ref[...] = out[:os_ref.shape[0]]


def _combine(pos_flat, h, weights, y_sorted, s_len):
    n, d = h.shape
    tt = _tile(s_len, COMBINE_TOKENS)
    n_s = n - s_len
    assert s_len % tt == 0 and 0 < n_s <= tt
    n_p_tiles = s_len // tt
    pos_flat = jnp.pad(pos_flat, (0, (n_p_tiles + 1) * tt * TOP_K - pos_flat.shape[0]))
    return pl.pallas_call(
        _combine_kernel,
        grid=(n_p_tiles + 1,),
        in_specs=[pl.BlockSpec((tt * TOP_K,), lambda i: (i,), memory_space=pltpu.SMEM),
                  pl.BlockSpec((tt, d), lambda i: (i, 0)),
                  pl.BlockSpec((tt, LANES), lambda i: (i, 0)),
                  pl.BlockSpec(memory_space=pl.ANY)],
        out_specs=[pl.BlockSpec((tt, d), lambda i: (jnp.minimum(i, n_p_tiles - 1), 0)),
                   pl.BlockSpec((n_s, d), lambda i: (0, 0))],
        out_shape=(jax.ShapeDtypeStruct((s_len, d), F32), jax.ShapeDtypeStruct((n_s, d), F32)),
        scratch_shapes=[pltpu.VMEM((TOP_K * tt, d), F32), pltpu.SemaphoreType.DMA(())],
        compiler_params=_params(("arbitrary",)),
        name="moe_combine",
    )(pos_flat, h, weights, y_sorted)


def _moe(h, s_len, norm2_g, w_router, b_router, w_gate_up, b_gate_up, w_down, b_down):
    n, d = h.shape
    n_experts = w_router.shape[1]
    hn, meta_i, meta_w, counts = _router(h, norm2_g, w_router, b_router)
    ids, rank = meta_i[:, :TOP_K], meta_i[:, TOP_K:2 * TOP_K]
    counts = counts[0, :n_experts]

    max_visits = (n * TOP_K) // MOE_VISIT_ROWS + n_experts
    n_rows = max_visits * MOE_VISIT_ROWS
    visits_per = (counts + MOE_VISIT_ROWS - 1) // MOE_VISIT_ROWS
    visit_end = jnp.cumsum(visits_per)
    visit_start = visit_end - visits_per
    total = visit_end[-1]
    v_eff = jnp.minimum(jnp.arange(max_visits, dtype=jnp.int32), total - 1)
    vis_expert = jnp.minimum(jnp.searchsorted(visit_end, v_eff, side="right"), n_experts - 1).astype(jnp.int32)
    vis_rows = jnp.clip(counts[vis_expert] - (v_eff - visit_start[vis_expert]) * MOE_VISIT_ROWS, 0, MOE_VISIT_ROWS)
    vis_rows = jnp.where(jnp.arange(max_visits) < total, vis_rows, 0).astype(jnp.int32)
    pos = (visit_start[ids] * MOE_VISIT_ROWS + rank).astype(jnp.int32)
    pos_flat = pos.reshape(-1)
    token = jnp.broadcast_to(jnp.arange(n, dtype=jnp.int32)[:, None], (n, TOP_K)).reshape(-1)
    src_rows = jnp.zeros((n_rows,), jnp.int32).at[pos_flat].set(token)
    chunks_per_visit = MOE_VISIT_ROWS // DISPATCH_ROWS
    chunk_valid = jnp.clip(jnp.repeat(vis_rows, chunks_per_visit)
                           - jnp.tile(jnp.arange(chunks_per_visit, dtype=jnp.int32) * DISPATCH_ROWS, max_visits),
                           0, DISPATCH_ROWS).astype(jnp.int32)

    xs = _dispatch(chunk_valid, src_rows, hn, n_rows)
    act = _gate_up(vis_expert, v_eff, vis_rows, xs, w_gate_up, b_gate_up)
    y_sorted = _down(vis_expert, v_eff, vis_rows, act, w_down, b_down)
    return _combine(pos_flat, h, meta_w, y_sorted, s_len)


def kernel(x_prompt, x_sample, cache_k, cache_v, cache_mem_k, cache_mem_v, page_table, mem_prompt, rel_bias, norm1_g, w_in, q_norm_g, k_norm_g, gmlp_ln_g, gmlp_ln_b, gmlp_ws, gmlp_bs, mem_norm_g, w_mem_kv, mq_norm_g, mk_norm_g, w_br_moba, w_br_gmlp, w_br_mem, w_out, norm2_g, w_router, b_router, w_gate_up, b_gate_up, w_down, b_down):
    batch, s_len, d = x_prompt.shape
    n_seq, dec_len, _ = x_sample.shape
    depth, n_phys, page, heads, hd = cache_k.shape
    _, _, mem, mem_heads, mem_hd = cache_mem_k.shape
    assert depth == 1 and batch == 1 and dec_len == 1
    moba_w, mem_w, gw = heads * hd, mem_heads * mem_hd, gmlp_ln_g.shape[-1]
    n_pages = page_table.shape[1]
    past_len = n_pages * page
    assert MOBA_BLOCK % page == 0 and past_len % MOBA_BLOCK == 0 and s_len % MOBA_BLOCK == 0
    ppb = MOBA_BLOCK // page
    nb = past_len // MOBA_BLOCK
    assert nb >= MOBA_TOPK
    n = s_len + n_seq

    x_all = jnp.concatenate([x_prompt.reshape(s_len, d), x_sample.reshape(n_seq, d)], axis=0)
    w_in2 = w_in.reshape(d, -1)
    xn = _rmsnorm(x_all, norm1_g, BF16)
    off = 0
    q = _proj(xn, w_in2, off, moba_w, "headnorm", q_norm_g, hd); off += moba_w
    k = _proj(xn, w_in2, off, moba_w, "headnorm", k_norm_g, hd); off += moba_w
    v = _proj(xn, w_in2, off, moba_w); off += moba_w
    z = _proj(xn, w_in2, off, 2 * gw, "gelu"); off += 2 * gw
    mq = _proj(xn, w_in2, off, mem_w, "headnorm", mq_norm_g, mem_hd); off += mem_w
    gates = _proj(xn, w_in2, off, 3 * d, "sigmoid")

    bias_tab = _bias_tables(rel_bias)
    att_p = _moba_prompt(q, k, v, bias_tab, s_len, heads, hd)
    q_s3, k_s3, v_s3 = (a[s_len:].reshape(n_seq, 1, moba_w) for a in (q, k, v))
    cache_k4 = cache_k.reshape(n_phys, page, heads, hd)
    cache_v4 = cache_v.reshape(n_phys, page, heads, hd)
    pt_flat = page_table.reshape(-1)
    sel = _sample_gate(pt_flat, q_s3.reshape(n_seq, heads, hd), cache_k4, n_seq, nb, ppb)
    sel_flat = jnp.swapaxes(sel[:, :MOBA_TOPK, :heads], 1, 2).reshape(-1)
    sample_bias = jnp.stack([bias_tab[:, 1, 0, :],
                             jnp.broadcast_to(bias_tab[:, 1, MOBA_BLOCK - 1, 0:1], (heads, MOBA_BLOCK)),
                             jnp.broadcast_to(bias_tab[:, 0, 0, 0:1], (heads, MOBA_BLOCK))], axis=1)
    att_s = _moba_sample(pt_flat, sel_flat, q_s3, k_s3, v_s3, sample_bias, cache_k4, cache_v4, n_seq, nb, ppb)
    att = jnp.concatenate([att_p, att_s.reshape(n_seq, moba_w)], axis=0)

    g_p, g_s, gv_s = _gmlp(z, s_len, gmlp_ln_g, gmlp_ln_b, gmlp_ws[0], gmlp_bs[0])
    g_out = jnp.concatenate([g_p, g_s], axis=0)

    mem_n = _rmsnorm(mem_prompt.reshape(mem, d), mem_norm_g, BF16)
    w_mem2 = w_mem_kv.reshape(d, 2 * mem_w)
    mk_p = _proj(mem_n, w_mem2, 0, mem_w, "headnorm", mk_norm_g, mem_hd)
    mv_p = _proj(mem_n, w_mem2, mem_w, mem_w)
    m_p = _mem_attn_prompt(mq, mk_p, mv_p, s_len, mem_heads)
    m_s = _mem_attn_sample(mq[s_len:].reshape(n_seq, 1, mem_w), cache_mem_k.reshape(n_seq, mem, mem_w),
                           cache_mem_v.reshape(n_seq, mem, mem_w), mem_heads)
    m_out = jnp.concatenate([m_p, m_s.reshape(n_seq, mem_w)], axis=0)

    mixed = _merge(att, g_out, m_out, gates, w_br_moba.reshape(moba_w, d), w_br_gmlp.reshape(gw, d), w_br_mem.reshape(mem_w, d))
    h = _proj(mixed, w_out.reshape(d, d), 0, d, "residual", x_all)
    n_experts = w_router.shape[-1]
    out_p, out_s = _moe(h, s_len, norm2_g, w_router.reshape(d, n_experts), b_router.reshape(n_experts),
                        w_gate_up.reshape(n_experts, d, -1), b_gate_up.reshape(n_experts, -1),
                        w_down.reshape(n_experts, -1, d), b_down.reshape(n_experts, d))

    y_prompt = out_p.reshape(batch, s_len, d)
    y_sample = out_s.reshape(n_seq, dec_len, d)
    k_prompt = k[:s_len].reshape(1, batch, s_len, heads, hd)
    v_prompt = v[:s_len].reshape(1, batch, s_len, heads, hd)
    mem_k_prompt = mk_p.reshape(1, batch, mem, mem_heads, mem_hd)
    mem_v_prompt = mv_p.reshape(1, batch, mem, mem_heads, mem_hd)
    k_sample = k[s_len:].reshape(1, n_seq, dec_len, heads, hd)
    v_sample = v[s_len:].reshape(1, n_seq, dec_len, heads, hd)
    gmlp_v_sample = gv_s.reshape(1, n_seq, dec_len, gmlp_ws.shape[1], -1)
    return (y_prompt, y_sample, k_prompt, v_prompt, mem_k_prompt, mem_v_prompt, k_sample, v_sample, gmlp_v_sample)
```

```python
import functools
import math

import numpy as np
import jax
import jax.numpy as jnp
from jax import lax
from jax.experimental import pallas as pl
from jax.experimental.pallas import tpu as pltpu

MOBA_BLOCK = 256
MOBA_TOPK = 3
REL_BUCKETS = 32
REL_MAX_DISTANCE = 128
TOP_K = 4
SWIGLU_LIMIT = 7.0
SWIGLU_ALPHA = 1.702
NORM_EPS = 1e-6

V7X_VMEM_BYTES = 64 * 1024 * 1024
VMEM_LIMIT = V7X_VMEM_BYTES - 12 * 1024 * 1024
LANES = 128

ROW_TILE = 1040
COL_TILE = 256
PROJ_COL_TILE = 512
NORM_ROWS = 256
MOBA_SUPER = 4
GATE_BLOCKS = 4
MOE_CHUNK = 512
MOE_TAIL = 128
MOE_COL_TILE = 512
MOE_VISIT_ROWS = 1280
DISPATCH_ROWS = 256
COMBINE_TOKENS = 64

F32 = jnp.float32
BF16 = jnp.bfloat16
HI = lax.Precision.HIGHEST
NT = (((1,), (1,)), ((), ()))
NEG_INF = float("-inf")
MASK_BIG = 2.0 ** 100


def _params(sem, vmem=VMEM_LIMIT):
    return pltpu.CompilerParams(dimension_semantics=sem, vmem_limit_bytes=vmem)


def _tile(n, pref):
    return n if n <= pref else pref


def _rmsnorm_kernel(x_ref, g_ref, o_ref):
    x = x_ref[...]
    y = x * lax.rsqrt(jnp.mean(x * x, axis=-1, keepdims=True) + NORM_EPS)
    o_ref[...] = (y * g_ref[...]).astype(o_ref.dtype)


def _rmsnorm(x, g, out_dtype):
    n, d = x.shape
    tm = _tile(n, NORM_ROWS)
    return pl.pallas_call(
        _rmsnorm_kernel,
        grid=(pl.cdiv(n, tm),),
        in_specs=[pl.BlockSpec((tm, d), lambda i: (i, 0)), pl.BlockSpec((1, d), lambda i: (0, 0))],
        out_specs=pl.BlockSpec((tm, d), lambda i: (i, 0)),
        out_shape=jax.ShapeDtypeStruct((n, d), out_dtype),
        compiler_params=_params(("parallel",)),
        name="rmsnorm",
    )(x, g.reshape(1, d))


def _proj_kernel(*refs, epilogue, head_dim):
    x_ref, w_ref, o_ref = refs[0], refs[1], refs[-1]
    acc = jnp.dot(x_ref[...], w_ref[...], preferred_element_type=F32)
    if epilogue == "headnorm":
        g = refs[2][...]
        for s in range(acc.shape[1] // head_dim):
            a = acc[:, s * head_dim:(s + 1) * head_dim]
            y = a * lax.rsqrt(jnp.mean(a * a, axis=-1, keepdims=True) + NORM_EPS)
            o_ref[:, s * head_dim:(s + 1) * head_dim] = (y * g).astype(o_ref.dtype)
    elif epilogue == "gelu":
        o_ref[...] = (acc * (lax.erf(acc * (1.0 / math.sqrt(2.0))) + 1.0) * 0.5).astype(o_ref.dtype)
    elif epilogue == "sigmoid":
        o_ref[...] = jax.nn.sigmoid(acc).astype(o_ref.dtype)
    elif epilogue == "residual":
        o_ref[...] = (refs[2][...] + acc).astype(o_ref.dtype)
    else:
        o_ref[...] = acc.astype(o_ref.dtype)


def _proj(x, w, col_off, width, epilogue="none", extra=None, head_dim=0, out_dtype=F32):
    n, k = x.shape
    tm = _tile(n, ROW_TILE)
    tn = _tile(width, PROJ_COL_TILE)
    while width % tn or col_off % tn:
        tn //= 2
    assert tn % LANES == 0
    joff = col_off // tn
    in_specs = [pl.BlockSpec((tm, k), lambda i, j: (i, 0)), pl.BlockSpec((k, tn), lambda i, j: (0, joff + j))]
    args = [x, w]
    if epilogue == "headnorm":
        assert tn % head_dim == 0
        in_specs.append(pl.BlockSpec((1, head_dim), lambda i, j: (0, 0)))
        args.append(extra.reshape(1, head_dim))
    elif epilogue == "residual":
        in_specs.append(pl.BlockSpec((tm, tn), lambda i, j: (i, j)))
        args.append(extra)
    return pl.pallas_call(
        functools.partial(_proj_kernel, epilogue=epilogue, head_dim=head_dim),
        grid=(pl.cdiv(n, tm), width // tn),
        in_specs=in_specs,
        out_specs=pl.BlockSpec((tm, tn), lambda i, j: (i, j)),
        out_shape=jax.ShapeDtypeStruct((n, width), out_dtype),
        compiler_params=_params(("parallel", "arbitrary")),
        name="proj_" + epilogue,
    )(*args)


def _t5_bucket_np(dist):
    n = np.maximum(dist, 0)
    max_exact = REL_BUCKETS // 2
    log_ratio = (np.log(np.maximum(n, 1).astype(np.float32) / np.float32(max_exact))
                 / np.float32(math.log(REL_MAX_DISTANCE / max_exact)))
    large = max_exact + (log_ratio * np.float32(REL_BUCKETS - max_exact)).astype(np.int32)
    return np.where(n < max_exact, n, np.minimum(large, REL_BUCKETS - 1)).astype(np.int32)


def _bias_table_kernel(rel_ref, bucket_ref, o_ref):
    h = pl.program_id(0)
    for t in range(2):
        bucket = bucket_ref[t]
        acc = jnp.zeros(bucket.shape, F32)
        for b in range(REL_BUCKETS):
            acc = jnp.where(bucket == b, rel_ref[b, h], acc)
        o_ref[0, t] = acc - rel_ref[REL_BUCKETS - 1, h]


def _bias_tables(rel_bias):
    assert MOBA_BLOCK >= REL_MAX_DISTANCE
    heads = rel_bias.shape[1]
    i = np.arange(MOBA_BLOCK)[:, None]
    j = np.arange(MOBA_BLOCK)[None, :]
    buckets = jnp.asarray(np.stack([_t5_bucket_np(i - j), _t5_bucket_np(MOBA_BLOCK + i - j)]))
    return pl.pallas_call(
        _bias_table_kernel,
        grid=(heads,),
        in_specs=[pl.BlockSpec(memory_space=pltpu.SMEM),
                  pl.BlockSpec((2, MOBA_BLOCK, MOBA_BLOCK), lambda h: (0, 0, 0))],
        out_specs=pl.BlockSpec((1, 2, MOBA_BLOCK, MOBA_BLOCK), lambda h: (h, 0, 0, 0)),
        out_shape=jax.ShapeDtypeStruct((heads, 2, MOBA_BLOCK, MOBA_BLOCK), F32),
        compiler_params=_params(("arbitrary",)),
        name="bias_tables",
    )(rel_bias, buckets)


def _top_blocks(gate, blk_iota, n_blocks, axis):
    picked = []
    mask = jnp.zeros(gate.shape, jnp.bool_)
    g = gate
    for _ in range(MOBA_TOPK):
        m = jnp.max(g, axis=axis, keepdims=True)
        first = jnp.min(jnp.where((g == m) & (m > NEG_INF), blk_iota, n_blocks), axis=axis, keepdims=True)
        pick = blk_iota == first
        picked.append(first)
        mask = mask | pick
        g = jnp.where(pick, NEG_INF, g)
    return picked, mask


def _moba_prompt_kernel(q_ref, k_ref, v_ref, bias_ref, o_ref, kmean_ref, kaug_ref, vbf_ref, s_ref, *, scale):
    h, qb = pl.program_id(0), pl.program_id(1)
    blk = MOBA_BLOCK
    s_len, hd = k_ref.shape
    nb = s_len // blk

    @pl.when((h == 0) & (qb == 0))
    def _():
        key_blk = lax.broadcasted_iota(jnp.int32, (s_len, LANES), 0) // blk
        lane = lax.broadcasted_iota(jnp.int32, (s_len, LANES), 1)
        kaug_ref[:, hd:] = jnp.where(key_blk == lane, -MASK_BIG, 0.0).astype(BF16)

    @pl.when(qb == 0)
    def _():
        kmean_ref[...] = jnp.zeros(kmean_ref.shape, F32)
        kmean_ref[:nb, :] = jnp.mean(k_ref[...].reshape(nb, blk, hd), axis=1)
        kaug_ref[:, :hd] = k_ref[...].astype(BF16)
        vbf_ref[...] = v_ref[...].astype(BF16)

    q = q_ref[...]
    gate_t = lax.dot_general(kmean_ref[...], q, NT, precision=HI, preferred_element_type=F32)
    blk_iota_t = lax.broadcasted_iota(jnp.int32, (LANES, blk), 0)
    _, sel_t = _top_blocks(jnp.where(blk_iota_t < qb, gate_t, NEG_INF), blk_iota_t, LANES, axis=0)
    sel_f = jnp.transpose(sel_t.astype(F32))
    blk_iota = lax.broadcasted_iota(jnp.int32, (blk, LANES), 1)

    qs = (q * scale).astype(BF16)
    row = lax.broadcasted_iota(jnp.int32, (blk, blk), 0)
    col = lax.broadcasted_iota(jnp.int32, (blk, blk), 1)

    def logits(start, rows):
        return lax.dot_general(qs, kaug_ref[pl.ds(start, rows), :hd], NT, preferred_element_type=F32)

    def pv(p, start, rows):
        return jnp.dot(p.astype(BF16), vbf_ref[pl.ds(start, rows), :], preferred_element_type=F32)

    def chose(kb):
        return jnp.max(jnp.where(blk_iota == kb, sel_f, 0.0), axis=1, keepdims=True) > 0.0

    def update(carry, s, start, rows):
        m, l, acc = carry
        m_new = jnp.maximum(m, jnp.max(s, axis=1, keepdims=True))
        alpha = jnp.exp(m - m_new)
        p = jnp.exp(s - m_new)
        return m_new, alpha * l + jnp.sum(p, axis=1, keepdims=True), alpha * acc + pv(p, start, rows)

    first = qb == 0
    prev_blk = jnp.maximum(qb - 1, 0)
    near = pl.multiple_of(prev_blk * blk, blk)
    raw = logits(near, 2 * blk)
    causal = jnp.where(row >= col, 0.0, NEG_INF)
    picked_prev = jnp.where(chose(prev_blk), 0.0, NEG_INF)
    s_lo = raw[:, :blk] + jnp.where(first, bias_ref[0, 0] + causal, bias_ref[0, 1] + picked_prev)
    s_hi = raw[:, blk:] + jnp.where(first, NEG_INF, bias_ref[0, 0] + causal)
    s = jnp.concatenate([s_lo, s_hi], axis=1)
    m = jnp.max(s, axis=1, keepdims=True)
    p = jnp.exp(s - m)
    carry = (m, jnp.sum(p, axis=1, keepdims=True), pv(p, near, 2 * blk))

    blocked = jnp.where((sel_f > 0.0) & (blk_iota < qb - 1), 0.0, 1.0)
    q_aug = jnp.concatenate([qs, blocked.astype(BF16)], axis=1)
    span = MOBA_SUPER * blk

    def masked_logits(g):
        start = pl.multiple_of(g * span, span)
        return lax.dot_general(q_aug, kaug_ref[pl.ds(start, span), :], NT, preferred_element_type=F32)

    def body(g, carry):
        nxt = masked_logits(g + 1)
        carry = update(carry, s_ref[...], pl.multiple_of(g * span, span), span)
        s_ref[...] = nxt
        return carry

    last = jnp.maximum((jnp.maximum(qb - 1, 0) + MOBA_SUPER - 1) // MOBA_SUPER - 1, 0)
    s_ref[...] = masked_logits(0)
    carry = lax.fori_loop(0, last, body, carry)
    m, l, acc = update(carry, s_ref[...], pl.multiple_of(last * span, span), span)
    o_ref[...] = (acc / l).astype(o_ref.dtype)


def _moba_prompt(q, k, v, bias_tab, s_len, heads, hd):
    assert s_len % (MOBA_SUPER * MOBA_BLOCK) == 0 and s_len // MOBA_BLOCK <= LANES
    nqb = s_len // MOBA_BLOCK
    return pl.pallas_call(
        functools.partial(_moba_prompt_kernel, scale=hd ** -0.5),
        grid=(heads, nqb),
        in_specs=[pl.BlockSpec((MOBA_BLOCK, hd), lambda h, i: (i, h)),
                  pl.BlockSpec((s_len, hd), lambda h, i: (0, h)),
                  pl.BlockSpec((s_len, hd), lambda h, i: (0, h)),
                  pl.BlockSpec((1, 2, MOBA_BLOCK, MOBA_BLOCK), lambda h, i: (h, 0, 0, 0))],
        out_specs=pl.BlockSpec((MOBA_BLOCK, hd), lambda h, i: (i, h)),
        out_shape=jax.ShapeDtypeStruct((s_len, heads * hd), BF16),
        scratch_shapes=[pltpu.VMEM((LANES, hd), F32), pltpu.VMEM((s_len, hd + LANES), BF16),
                        pltpu.VMEM((s_len, hd), BF16), pltpu.VMEM((MOBA_BLOCK, MOBA_SUPER * MOBA_BLOCK), F32)],
        compiler_params=_params(("arbitrary", "arbitrary")),
        name="moba_prompt",
    )(q, k, v, bias_tab)


def _sample_gate_kernel(pt_ref, q_ref, *refs, ppb, bps):
    page_refs, sel_ref, kmean_ref = refs[:bps * ppb], refs[bps * ppb], refs[bps * ppb + 1]
    j = pl.program_id(1)
    nb, heads, _ = kmean_ref.shape
    for t in range(bps):
        total = jnp.sum(page_refs[t * ppb][0], axis=0)
        for p in range(1, ppb):
            total = total + jnp.sum(page_refs[t * ppb + p][0], axis=0)
        kmean_ref[j * bps + t] = total * (1.0 / MOBA_BLOCK)

    @pl.when(j == pl.num_programs(1) - 1)
    def _():
        gate = jnp.sum(kmean_ref[...] * q_ref[...], axis=-1)
        blk_iota = lax.broadcasted_iota(jnp.int32, (nb, heads), 0)
        picked, _ = _top_blocks(gate, blk_iota, nb, axis=0)
        sel_ref[...] = jnp.zeros(sel_ref.shape, jnp.int32)
        for t in range(MOBA_TOPK):
            sel_ref[0, t:t + 1, :heads] = picked[t]


def _sample_gate(page_table_flat, q_s, cache_k, n_seq, nb, ppb):
    _, page, heads, hd = cache_k.shape
    bps = _tile(nb, GATE_BLOCKS)
    assert nb % bps == 0 and heads <= LANES and MOBA_TOPK <= 8
    n_pages = nb * ppb
    page_specs = [pl.BlockSpec((1, page, heads, hd),
                               lambda b, j, pt, i=i: (pt[b * n_pages + j * bps * ppb + i], 0, 0, 0))
                  for i in range(bps * ppb)]
    return pl.pallas_call(
        functools.partial(_sample_gate_kernel, ppb=ppb, bps=bps),
        grid_spec=pltpu.PrefetchScalarGridSpec(
            num_scalar_prefetch=1,
            grid=(n_seq, nb // bps),
            in_specs=[pl.BlockSpec((1, heads, hd), lambda b, j, pt: (b, 0, 0))] + page_specs,
            out_specs=pl.BlockSpec((1, 8, LANES), lambda b, j, pt: (b, 0, 0)),
            scratch_shapes=[pltpu.VMEM((nb, heads, hd), F32)]),
        out_shape=jax.ShapeDtypeStruct((n_seq, 8, LANES), jnp.int32),
        compiler_params=_params(("parallel", "arbitrary")),
        name="moba_sample_gate",
    )(page_table_flat, q_s, *([cache_k] * (bps * ppb)))


def _moba_sample_kernel(pt_ref, sel_ref, q_ref, kn_ref, vn_ref, sb_ref, ck_ref, cv_ref, o_ref, kbuf, vbuf, sems,
                        *, nb, ppb, scale):
    b = pl.program_id(0)
    _, heads, n_keys, hd = kbuf.shape
    page = n_keys // (MOBA_TOPK * ppb)
    n_pages = nb * ppb
    pages = [(h, t, p) for h in range(heads) for t in range(MOBA_TOPK) for p in range(ppb)]

    def page_copies(seq, h, t, p):
        half = seq % 2
        phys = pt_ref[seq * n_pages + sel_ref[(seq * heads + h) * MOBA_TOPK + t] * ppb + p]
        dst = pl.ds((t * ppb + p) * page, page)
        return (pltpu.make_async_copy(ck_ref.at[phys, :, h, :], kbuf.at[half, h, dst, :], sems.at[half, 0]),
                pltpu.make_async_copy(cv_ref.at[phys, :, h, :], vbuf.at[half, h, dst, :], sems.at[half, 1]))

    def fetch(seq):
        for page_id in pages:
            for copy in page_copies(seq, *page_id):
                copy.start()

    @pl.when(b == 0)
    def _():
        fetch(b)

    @pl.when(b + 1 < pl.num_programs(0))
    def _():
        fetch(b + 1)

    for page_id in pages:
        for copy in page_copies(b, *page_id):
            copy.wait()

    cur = b % 2
    for h in range(heads):
        cols = slice(h * hd, (h + 1) * hd)
        q = q_ref[0, :, cols]
        q8 = jnp.broadcast_to(q, (8, hd)).astype(BF16)
        near, far, self_bias = sb_ref[h, 0:1, :], sb_ref[h, 1:2, :], sb_ref[h, 2:3, 0:1]
        bias = jnp.concatenate(
            [jnp.where(sel_ref[(b * heads + h) * MOBA_TOPK + t] == nb - 1, near, far) for t in range(MOBA_TOPK)], axis=1)
        s = lax.dot_general(q8, kbuf[cur, h].astype(BF16), NT, preferred_element_type=F32)[0:1] * scale + bias
        s_self = jnp.sum(q * kn_ref[0, :, cols], axis=1, keepdims=True) * scale + self_bias
        m = jnp.maximum(jnp.max(s, axis=1, keepdims=True), s_self)
        p = jnp.exp(s - m)
        p_self = jnp.exp(s_self - m)
        l = jnp.sum(p, axis=1, keepdims=True) + p_self
        p8 = jnp.broadcast_to(p, (8, n_keys)).astype(BF16)
        acc = jnp.dot(p8, vbuf[cur, h].astype(BF16), preferred_element_type=F32)[0:1] + p_self * vn_ref[0, :, cols]
        o_ref[0, :, cols] = (acc / l).astype(o_ref.dtype)


def _moba_sample(page_table_flat, sel_flat, q_s3, k_s3, v_s3, sample_bias, cache_k, cache_v, n_seq, nb, ppb):
    _, page, heads, hd = cache_k.shape
    n_keys = MOBA_TOPK * ppb * page
    row_spec = pl.BlockSpec((1, 1, heads * hd), lambda b, pt, sel: (b, 0, 0))
    return pl.pallas_call(
        functools.partial(_moba_sample_kernel, nb=nb, ppb=ppb, scale=hd ** -0.5),
        grid_spec=pltpu.PrefetchScalarGridSpec(
            num_scalar_prefetch=2,
            grid=(n_seq,),
            in_specs=[row_spec, row_spec, row_spec,
                      pl.BlockSpec((heads, 3, MOBA_BLOCK), lambda b, pt, sel: (0, 0, 0)),
                      pl.BlockSpec(memory_space=pl.ANY), pl.BlockSpec(memory_space=pl.ANY)],
            out_specs=row_spec,
            scratch_shapes=[pltpu.VMEM((2, heads, n_keys, hd), F32), pltpu.VMEM((2, heads, n_keys, hd), F32),
                            pltpu.SemaphoreType.DMA((2, 2))]),
        out_shape=jax.ShapeDtypeStruct((n_seq, 1, heads * hd), BF16),
        compiler_params=_params(("arbitrary",)),
        name="moba_sample",
    )(page_table_flat, sel_flat, q_s3, k_s3, v_s3, sample_bias, cache_k, cache_v)


def _layernorm(x, g, b):
    mu = jnp.mean(x, axis=-1, keepdims=True)
    var = jnp.mean(jnp.square(x - mu), axis=-1, keepdims=True)
    return (x - mu) * lax.rsqrt(var + NORM_EPS) * g + b


def _gmlp_prompt_kernel(z_ref, g_ref, b_ref, ws_ref, bs_ref, o_ref):
    groups, chunk, _ = ws_ref.shape
    gw = g_ref.shape[1]
    gd = gw // groups
    u = z_ref[:, :gw]
    gv = _layernorm(z_ref[:, gw:], g_ref[...], b_ref[...]).astype(BF16)
    row = lax.broadcasted_iota(jnp.int32, (chunk, chunk), 0)
    col = lax.broadcasted_iota(jnp.int32, (chunk, chunk), 1)
    for g in range(groups):
        ws = jnp.where(row >= col, ws_ref[g], 0.0).astype(BF16)
        mixed = jnp.dot(ws, gv[:, g * gd:(g + 1) * gd], preferred_element_type=F32) + bs_ref[:, g:g + 1]
        o_ref[:, g * gd:(g + 1) * gd] = (u[:, g * gd:(g + 1) * gd] * mixed).astype(o_ref.dtype)


def _gmlp_sample_kernel(z_ref, g_ref, b_ref, w0_ref, b0_ref, o_ref, gv_ref):
    gw = g_ref.shape[1]
    gv = _layernorm(z_ref[:, gw:], g_ref[...], b_ref[...])
    gv_ref[...] = gv
    o_ref[...] = (z_ref[:, :gw] * (w0_ref[...] * gv + b0_ref[...])).astype(o_ref.dtype)


def _gmlp(z, s_len, ln_g, ln_b, ws, bs):
    n, two_gw = z.shape
    gw = two_gw // 2
    groups, chunk, _ = ws.shape
    gd = gw // groups
    assert s_len % chunk == 0
    g2, b2 = ln_g.reshape(1, gw), ln_b.reshape(1, gw)
    out_p = pl.pallas_call(
        _gmlp_prompt_kernel,
        grid=(s_len // chunk,),
        in_specs=[pl.BlockSpec((chunk, two_gw), lambda c: (c, 0)),
                  pl.BlockSpec((1, gw), lambda c: (0, 0)), pl.BlockSpec((1, gw), lambda c: (0, 0)),
                  pl.BlockSpec((groups, chunk, chunk), lambda c: (0, 0, 0)),
                  pl.BlockSpec((chunk, groups), lambda c: (0, 0))],
        out_specs=pl.BlockSpec((chunk, gw), lambda c: (c, 0)),
        out_shape=jax.ShapeDtypeStruct((s_len, gw), BF16),
        compiler_params=_params(("parallel",)),
        name="gmlp_prompt",
    )(z, g2, b2, ws, bs.T)
    z_s = z[s_len:]
    n_s = n - s_len
    w0 = jnp.repeat(ws[:, 0, 0], gd).reshape(1, gw)
    b0 = jnp.repeat(bs[:, 0], gd).reshape(1, gw)
    out_s, gv_s = pl.pallas_call(
        _gmlp_sample_kernel,
        out_shape=(jax.ShapeDtypeStruct((n_s, gw), BF16), jax.ShapeDtypeStruct((n_s, gw), F32)),
        name="gmlp_sample",
    )(z_s, g2, b2, w0, b0)
    return out_p, out_s, gv_s


def _mem_attn_kernel(q_ref, k_ref, v_ref, o_ref, *, heads, scale, batched):
    q = q_ref[0] if batched else q_ref[...]
    k = k_ref[0] if batched else k_ref[...]
    v = v_ref[0] if batched else v_ref[...]
    rows = max(q.shape[0], 8)
    hd = q.shape[1] // heads
    for h in range(heads):
        sl = slice(h * hd, (h + 1) * hd)
        qh = jnp.broadcast_to(q[:, sl], (rows, hd)).astype(BF16)
        s = lax.dot_general(qh, k[:, sl].astype(BF16), NT, preferred_element_type=F32) * scale
        p = jnp.exp(s - jnp.max(s, axis=1, keepdims=True))
        l = jnp.sum(p, axis=1, keepdims=True)
        out = jnp.dot(p.astype(BF16), v[:, sl].astype(BF16), preferred_element_type=F32) / l
        out = out[:q.shape[0]].astype(o_ref.dtype)
        if batched:
            o_ref[0, :, sl] = out
        else:
            o_ref[:, sl] = out


def _mem_attn_prompt(mq, mk, mv, s_len, heads):
    width = mq.shape[1]
    mem = mk.shape[0]
    tq = _tile(s_len, 512)
    return pl.pallas_call(
        functools.partial(_mem_attn_kernel, heads=heads, scale=(width // heads) ** -0.5, batched=False),
        grid=(s_len // tq,),
        in_specs=[pl.BlockSpec((tq, width), lambda i: (i, 0)),
                  pl.BlockSpec((mem, width), lambda i: (0, 0)), pl.BlockSpec((mem, width), lambda i: (0, 0))],
        out_specs=pl.BlockSpec((tq, width), lambda i: (i, 0)),
        out_shape=jax.ShapeDtypeStruct((s_len, width), BF16),
        compiler_params=_params(("parallel",)),
        name="mem_attn_prompt",
    )(mq, mk, mv)


def _mem_attn_sample(mq_s3, ck, cv, heads):
    n_seq, mem, width = ck.shape
    return pl.pallas_call(
        functools.partial(_mem_attn_kernel, heads=heads, scale=(width // heads) ** -0.5, batched=True),
        grid=(n_seq,),
        in_specs=[pl.BlockSpec((1, 1, width), lambda b: (b, 0, 0)),
                  pl.BlockSpec((1, mem, width), lambda b: (b, 0, 0)), pl.BlockSpec((1, mem, width), lambda b: (b, 0, 0))],
        out_specs=pl.BlockSpec((1, 1, width), lambda b: (b, 0, 0)),
        out_shape=jax.ShapeDtypeStruct((n_seq, 1, width), BF16),
        compiler_params=_params(("parallel",)),
        name="mem_attn_sample",
    )(mq_s3, ck, cv)


def _merge_kernel(a_ref, g_ref, m_ref, ga_ref, gg_ref, gm_ref, wa_ref, wg_ref, wm_ref, o_ref):
    mixed = ga_ref[...] * jnp.dot(a_ref[...], wa_ref[...], preferred_element_type=F32)
    mixed = mixed + gg_ref[...] * jnp.dot(g_ref[...], wg_ref[...], preferred_element_type=F32)
    mixed = mixed + gm_ref[...] * jnp.dot(m_ref[...], wm_ref[...], preferred_element_type=F32)
    o_ref[...] = mixed.astype(o_ref.dtype)


def _merge(att, gout, mout, gates, w_a, w_g, w_m):
    n = att.shape[0]
    d = w_a.shape[1]
    tm = _tile(n, ROW_TILE)
    tn = _tile(d, COL_TILE)
    nj = d // tn

    def act_spec(a):
        return pl.BlockSpec((tm, a.shape[1]), lambda i, j: (i, 0))

    def gate_spec(branch):
        return pl.BlockSpec((tm, tn), lambda i, j: (i, branch * nj + j))

    def w_spec(w):
        return pl.BlockSpec((w.shape[0], tn), lambda i, j: (0, j))

    return pl.pallas_call(
        _merge_kernel,
        grid=(pl.cdiv(n, tm), nj),
        in_specs=[act_spec(att), act_spec(gout), act_spec(mout), gate_spec(0), gate_spec(1), gate_spec(2),
                  w_spec(w_a), w_spec(w_g), w_spec(w_m)],
        out_specs=pl.BlockSpec((tm, tn), lambda i, j: (i, j)),
        out_shape=jax.ShapeDtypeStruct((n, d), BF16),
        compiler_params=_params(("parallel", "arbitrary")),
        name="merge",
    )(att, gout, mout, gates, gates, gates, w_a, w_g, w_m)


def _router_kernel(h_ref, g_ref, wr_ref, br_ref, hn_ref, mi_ref, mw_ref, cnt_ref, carry_ref, *, n_tokens, n_experts):
    i = pl.program_id(0)
    tm = h_ref.shape[0]

    @pl.when(i == 0)
    def _():
        carry_ref[...] = jnp.zeros(carry_ref.shape, F32)

    x = h_ref[...]
    hn = x * lax.rsqrt(jnp.mean(x * x, axis=-1, keepdims=True) + NORM_EPS) * g_ref[...]
    hn_ref[...] = hn
    logits = jnp.dot(hn, wr_ref[...], precision=HI, preferred_element_type=F32) + br_ref[...]
    e_iota = lax.broadcasted_iota(jnp.int32, (tm, n_experts), 1)
    vals, ids = [], []
    l = logits
    for _ in range(TOP_K):
        m = jnp.max(l, axis=1, keepdims=True)
        first = jnp.min(jnp.where(l == m, e_iota, n_experts), axis=1, keepdims=True)
        vals.append(m)
        ids.append(first)
        l = jnp.where(e_iota == first, NEG_INF, l)
    ex = [jnp.exp(v - vals[0]) for v in vals]
    denom = ex[0]
    for e in ex[1:]:
        denom = denom + e
    valid = (i * tm + lax.broadcasted_iota(jnp.int32, (tm, 1), 0)) < n_tokens
    multi_hot = jnp.zeros((tm, n_experts), F32)
    for k in range(TOP_K):
        multi_hot = multi_hot + (e_iota == ids[k]).astype(F32)
    multi_hot = jnp.where(valid, multi_hot, 0.0)
    lower = (lax.broadcasted_iota(jnp.int32, (tm, tm), 0) > lax.broadcasted_iota(jnp.int32, (tm, tm), 1)).astype(BF16)
    before = jnp.dot(lower, multi_hot.astype(BF16), preferred_element_type=F32) + carry_ref[:, :n_experts]
    lane = lax.broadcasted_iota(jnp.int32, (tm, LANES), 1)
    meta_i = jnp.zeros((tm, LANES), jnp.int32)
    meta_w = jnp.zeros((tm, LANES), F32)
    for k in range(TOP_K):
        rank = jnp.sum(jnp.where(e_iota == ids[k], before, 0.0), axis=1, keepdims=True).astype(jnp.int32)
        meta_i = jnp.where(lane == k, ids[k], meta_i)
        meta_i = jnp.where(lane == TOP_K + k, rank, meta_i)
        meta_w = jnp.where(lane == k, ex[k] / denom, meta_w)
    mi_ref[...] = meta_i
    mw_ref[...] = meta_w
    carry_ref[:, :n_experts] = carry_ref[:, :n_experts] + jnp.sum(multi_hot, axis=0, keepdims=True)
    cnt_ref[...] = carry_ref[...].astype(jnp.int32)


def _router(h, g, w_router, b_router):
    n, d = h.shape
    n_experts = w_router.shape[1]
    assert n_experts <= LANES
    tm = _tile(n, NORM_ROWS)
    return pl.pallas_call(
        functools.partial(_router_kernel, n_tokens=n, n_experts=n_experts),
        grid=(pl.cdiv(n, tm),),
        in_specs=[pl.BlockSpec((tm, d), lambda i: (i, 0)), pl.BlockSpec((1, d), lambda i: (0, 0)),
                  pl.BlockSpec((d, n_experts), lambda i: (0, 0)), pl.BlockSpec((1, n_experts), lambda i: (0, 0))],
        out_specs=[pl.BlockSpec((tm, d), lambda i: (i, 0)), pl.BlockSpec((tm, LANES), lambda i: (i, 0)),
                   pl.BlockSpec((tm, LANES), lambda i: (i, 0)), pl.BlockSpec((1, LANES), lambda i: (0, 0))],
        out_shape=(jax.ShapeDtypeStruct((n, d), F32), jax.ShapeDtypeStruct((n, LANES), jnp.int32),
                   jax.ShapeDtypeStruct((n, LANES), F32), jax.ShapeDtypeStruct((1, LANES), jnp.int32)),
        scratch_shapes=[pltpu.VMEM((1, LANES), F32)],
        compiler_params=_params(("arbitrary",)),
        name="router",
    )(h, g.reshape(1, d), w_router, b_router.reshape(1, n_experts))


def _dispatch_kernel(valid_ref, src_ref, src_next_ref, hn_ref, o_ref, buf_ref, sems):
    c = pl.program_id(0)

    def row_copy(chunk, r, src_row):
        half = chunk % 2
        return pltpu.make_async_copy(hn_ref.at[pl.ds(src_row, 1)], buf_ref.at[half, pl.ds(r, 1)], sems.at[half])

    def fetch(chunk, idx_ref):
        def start(r, carry):
            row_copy(chunk, r, idx_ref[r]).start()
            return carry

        lax.fori_loop(0, valid_ref[chunk], start, 0)

    @pl.when(c == 0)
    def _():
        buf_ref[...] = jnp.zeros(buf_ref.shape, buf_ref.dtype)
        fetch(c, src_ref)

    @pl.when(c + 1 < pl.num_programs(0))
    def _():
        fetch(c + 1, src_next_ref)

    @pl.when(valid_ref[c] > 0)
    def _():
        def wait(r, carry):
            row_copy(c, r, 0).wait()
            return carry

        lax.fori_loop(0, valid_ref[c], wait, 0)
        o_ref[...] = buf_ref[c % 2].astype(o_ref.dtype)

    @pl.when(valid_ref[c] == 0)
    def _():
        o_ref[...] = jnp.zeros(o_ref.shape, o_ref.dtype)


def _dispatch(chunk_valid, src_rows, hn, n_rows):
    d = hn.shape[1]
    n_chunks = n_rows // DISPATCH_ROWS
    return pl.pallas_call(
        _dispatch_kernel,
        grid_spec=pltpu.PrefetchScalarGridSpec(
            num_scalar_prefetch=1,
            grid=(n_chunks,),
            in_specs=[pl.BlockSpec((DISPATCH_ROWS,), lambda c, valid: (c,), memory_space=pltpu.SMEM),
                      pl.BlockSpec((DISPATCH_ROWS,), lambda c, valid: (jnp.minimum(c + 1, n_chunks - 1),),
                                   memory_space=pltpu.SMEM),
                      pl.BlockSpec(memory_space=pl.ANY)],
            out_specs=pl.BlockSpec((DISPATCH_ROWS, d), lambda c, valid: (c, 0)),
            scratch_shapes=[pltpu.VMEM((2, DISPATCH_ROWS, d), F32), pltpu.SemaphoreType.DMA((2,))]),
        out_shape=jax.ShapeDtypeStruct((n_rows, d), BF16),
        compiler_params=_params(("arbitrary",)),
        name="moe_dispatch",
    )(chunk_valid, src_rows, src_rows, hn)


def _visit_specs(n_col_tiles):
    def col(v, j, vr):
        return jnp.where(vr[v] > 0, j, n_col_tiles - 1)

    def rows(v, j, ve, vb, vr):
        return (vb[v], 0)

    def weight(v, j, ve, vb, vr):
        return (ve[v], 0, col(v, j, vr))

    def out(v, j, ve, vb, vr):
        return (v, j)

    return rows, weight, out


def _run_visit(n_rows, o_ref, compute, finish, stage_ref):
    units_per_chunk = MOE_CHUNK // MOE_TAIL
    full = n_rows // MOE_CHUNK
    tail_units = (n_rows - full * MOE_CHUNK + MOE_TAIL - 1) // MOE_TAIL
    merge = (full > 0) & (tail_units == 1)
    full = jnp.where(merge, full - 1, full)
    tail_units = jnp.where(merge, units_per_chunk + 1, tail_units)

    @pl.when((pl.program_id(0) == 0) & (pl.program_id(1) == 0))
    def _():
        stage_ref[...] = jnp.zeros(stage_ref.shape, stage_ref.dtype)

    @pl.when(full > 0)
    def _():
        stage_ref[...] = compute(0, MOE_CHUNK)

        def body(c, carry):
            r0 = pl.multiple_of(c * MOE_CHUNK, MOE_CHUNK)
            nxt = compute(r0, MOE_CHUNK)
            finish(stage_ref[...], pl.multiple_of(r0 - MOE_CHUNK, MOE_CHUNK), MOE_CHUNK)
            stage_ref[...] = nxt
            return carry

        lax.fori_loop(1, full, body, 0)

    staged_r = pl.multiple_of(jnp.maximum(full - 1, 0) * MOE_CHUNK, MOE_CHUNK)
    r_full = pl.multiple_of(full * MOE_CHUNK, MOE_CHUNK)
    for units in range(1, units_per_chunk + 2):
        @pl.when(tail_units == units)
        def _(rows=units * MOE_TAIL):
            tail = compute(r_full, rows)
            finish(stage_ref[...], staged_r, MOE_CHUNK)
            finish(tail, r_full, rows)

    @pl.when((tail_units == 0) & (full > 0))
    def _():
        finish(stage_ref[...], staged_r, MOE_CHUNK)

    def zero(c, carry):
        o_ref[pl.ds(pl.multiple_of(c * MOE_TAIL, MOE_TAIL), MOE_TAIL), :] = jnp.zeros((MOE_TAIL, o_ref.shape[1]), o_ref.dtype)
        return carry

    lax.fori_loop(full * units_per_chunk + tail_units, o_ref.shape[0] // MOE_TAIL, zero, 0)


def _gate_up_kernel(ve_ref, vb_ref, vr_ref, x_ref, w_ref, b_ref, p_ref, o_ref, wb_ref, stage_ref):
    @pl.when(vr_ref[pl.program_id(0)] > 0)
    def _():
        wb_ref[...] = w_ref[0].astype(BF16)

    tn = wb_ref.shape[1]

    def compute(r0, rows):
        return jnp.dot(x_ref[pl.ds(r0, rows), :], wb_ref[...], preferred_element_type=F32) + b_ref[0]

    def finish(gu, r0, rows):
        nxt = pltpu.roll(gu, tn - 1, 1)
        glu = jnp.minimum(gu, SWIGLU_LIMIT)
        lin = jnp.clip(nxt, -SWIGLU_LIMIT, SWIGLU_LIMIT)
        even = (lax.broadcasted_iota(jnp.int32, (rows, tn), 1) % 2) == 0
        act = jnp.where(even, glu * jax.nn.sigmoid(SWIGLU_ALPHA * glu) * (lin + 1.0), 0.0)
        o_ref[pl.ds(r0, rows), :] = jnp.dot(act.astype(BF16), p_ref[...], preferred_element_type=F32).astype(o_ref.dtype)

    _run_visit(vr_ref[pl.program_id(0)], o_ref, compute, finish, stage_ref)


def _gate_up(vis_expert, vis_block, vis_rows, xs, w_gate_up, b_gate_up):
    n_rows, d = xs.shape
    n_experts, _, two_f = w_gate_up.shape
    tn = _tile(two_f, MOE_COL_TILE)
    sel = np.zeros((tn, tn // 2), np.float32)
    sel[2 * np.arange(tn // 2), np.arange(tn // 2)] = 1.0
    rows, weight, out = _visit_specs(two_f // tn)
    return pl.pallas_call(
        _gate_up_kernel,
        grid_spec=pltpu.PrefetchScalarGridSpec(
            num_scalar_prefetch=3,
            grid=(n_rows // MOE_VISIT_ROWS, two_f // tn),
            in_specs=[pl.BlockSpec((MOE_VISIT_ROWS, d), rows),
                      pl.BlockSpec((1, d, tn), weight),
                      pl.BlockSpec((1, 1, tn), weight),
                      pl.BlockSpec((tn, tn // 2), lambda v, j, ve, vb, vr: (0, 0))],
            out_specs=pl.BlockSpec((MOE_VISIT_ROWS, tn // 2), out),
            scratch_shapes=[pltpu.VMEM((d, tn), BF16), pltpu.VMEM((MOE_CHUNK, tn), F32)]),
        out_shape=jax.ShapeDtypeStruct((n_rows, two_f // 2), BF16),
        compiler_params=_params(("arbitrary", "arbitrary")),
        name="moe_gate_up",
    )(vis_expert, vis_block, vis_rows, xs, w_gate_up, b_gate_up.reshape(n_experts, 1, two_f), jnp.asarray(sel, BF16))


def _down_kernel(ve_ref, vb_ref, vr_ref, x_ref, w_ref, b_ref, o_ref, wb_ref, stage_ref):
    @pl.when(vr_ref[pl.program_id(0)] > 0)
    def _():
        wb_ref[...] = w_ref[0].astype(BF16)

    def compute(r0, rows):
        return jnp.dot(x_ref[pl.ds(r0, rows), :], wb_ref[...], preferred_element_type=F32) + b_ref[0]

    def finish(y, r0, rows):
        o_ref[pl.ds(r0, rows), :] = y

    _run_visit(vr_ref[pl.program_id(0)], o_ref, compute, finish, stage_ref)


def _down(vis_expert, vis_block, vis_rows, act, w_down, b_down):
    n_rows, f = act.shape
    n_experts, _, d = w_down.shape
    tn = _tile(d, MOE_COL_TILE)
    rows, weight, out = _visit_specs(d // tn)
    return pl.pallas_call(
        _down_kernel,
        grid_spec=pltpu.PrefetchScalarGridSpec(
            num_scalar_prefetch=3,
            grid=(n_rows // MOE_VISIT_ROWS, d // tn),
            in_specs=[pl.BlockSpec((MOE_VISIT_ROWS, f), rows),
                      pl.BlockSpec((1, f, tn), weight),
                      pl.BlockSpec((1, 1, tn), weight)],
            out_specs=pl.BlockSpec((MOE_VISIT_ROWS, tn), out),
            scratch_shapes=[pltpu.VMEM((f, tn), BF16), pltpu.VMEM((MOE_CHUNK, tn), F32)]),
        out_shape=jax.ShapeDtypeStruct((n_rows, d), F32),
        compiler_params=_params(("arbitrary", "arbitrary")),
        name="moe_down",
    )(vis_expert, vis_block, vis_rows, act, w_down, b_down.reshape(n_experts, 1, d))


def _combine_kernel(pos_ref, h_ref, w_ref, y_ref, op_ref, os_ref, buf_ref, sem):
    tt = h_ref.shape[0]

    def row_copy(t, k, src_row):
        return pltpu.make_async_copy(y_ref.at[pl.ds(src_row, 1)], buf_ref.at[pl.ds(k * tt + t, 1)], sem)

    def start(t, carry):
        for k in range(TOP_K):
            row_copy(t, k, pos_ref[t * TOP_K + k]).start()
        return carry

    def wait(t, carry):
        for k in range(TOP_K):
            row_copy(t, k, 0).wait()
        return carry

    lax.fori_loop(0, tt, start, 0)
    lax.fori_loop(0, tt, wait, 0)
    out = h_ref[...]
    for k in range(TOP_K):
        out = out + w_ref[:, k:k + 1] * buf_ref[k * tt:(k + 1) * tt, :]
    is_prompt = pl.program_id(0) < pl.num_programs(0) - 1

    @pl.when(is_prompt)
    def _():
        op_ref[...] = out

    @pl.when(jnp.logical_not(is_prompt))
    def _():
        os_ref[...] = out[:os_ref.shape[0]]


def _combine(pos_flat, h, weights, y_sorted, s_len):
    n, d = h.shape
    tt = _tile(s_len, COMBINE_TOKENS)
    n_s = n - s_len
    assert s_len % tt == 0 and 0 < n_s <= tt
    n_p_tiles = s_len // tt
    pos_flat = jnp.pad(pos_flat, (0, (n_p_tiles + 1) * tt * TOP_K - pos_flat.shape[0]))
    return pl.pallas_call(
        _combine_kernel,
        grid=(n_p_tiles + 1,),
        in_specs=[pl.BlockSpec((tt * TOP_K,), lambda i: (i,), memory_space=pltpu.SMEM),
                  pl.BlockSpec((tt, d), lambda i: (i, 0)),
                  pl.BlockSpec((tt, LANES), lambda i: (i, 0)),
                  pl.BlockSpec(memory_space=pl.ANY)],
        out_specs=[pl.BlockSpec((tt, d), lambda i: (jnp.minimum(i, n_p_tiles - 1), 0)),
                   pl.BlockSpec((n_s, d), lambda i: (0, 0))],
        out_shape=(jax.ShapeDtypeStruct((s_len, d), F32), jax.ShapeDtypeStruct((n_s, d), F32)),
        scratch_shapes=[pltpu.VMEM((TOP_K * tt, d), F32), pltpu.SemaphoreType.DMA(())],
        compiler_params=_params(("arbitrary",)),
        name="moe_combine",
    )(pos_flat, h, weights, y_sorted)


def _moe(h, s_len, norm2_g, w_router, b_router, w_gate_up, b_gate_up, w_down, b_down):
    n, d = h.shape
    n_experts = w_router.shape[1]
    hn, meta_i, meta_w, counts = _router(h, norm2_g, w_router, b_router)
    ids, rank = meta_i[:, :TOP_K], meta_i[:, TOP_K:2 * TOP_K]
    counts = counts[0, :n_experts]

    max_visits = (n * TOP_K) // MOE_VISIT_ROWS + n_experts
    n_rows = max_visits * MOE_VISIT_ROWS
    visits_per = (counts + MOE_VISIT_ROWS - 1) // MOE_VISIT_ROWS
    visit_end = jnp.cumsum(visits_per)
    visit_start = visit_end - visits_per
    total = visit_end[-1]
    v_eff = jnp.minimum(jnp.arange(max_visits, dtype=jnp.int32), total - 1)
    vis_expert = jnp.minimum(jnp.searchsorted(visit_end, v_eff, side="right"), n_experts - 1).astype(jnp.int32)
    vis_rows = jnp.clip(counts[vis_expert] - (v_eff - visit_start[vis_expert]) * MOE_VISIT_ROWS, 0, MOE_VISIT_ROWS)
    vis_rows = jnp.where(jnp.arange(max_visits) < total, vis_rows, 0).astype(jnp.int32)
    pos = (visit_start[ids] * MOE_VISIT_ROWS + rank).astype(jnp.int32)
    pos_flat = pos.reshape(-1)
    token = jnp.broadcast_to(jnp.arange(n, dtype=jnp.int32)[:, None], (n, TOP_K)).reshape(-1)
    src_rows = jnp.zeros((n_rows,), jnp.int32).at[pos_flat].set(token)
    chunks_per_visit = MOE_VISIT_ROWS // DISPATCH_ROWS
    chunk_valid = jnp.clip(jnp.repeat(vis_rows, chunks_per_visit)
                           - jnp.tile(jnp.arange(chunks_per_visit, dtype=jnp.int32) * DISPATCH_ROWS, max_visits),
                           0, DISPATCH_ROWS).astype(jnp.int32)

    xs = _dispatch(chunk_valid, src_rows, hn, n_rows)
    act = _gate_up(vis_expert, v_eff, vis_rows, xs, w_gate_up, b_gate_up)
    y_sorted = _down(vis_expert, v_eff, vis_rows, act, w_down, b_down)
    return _combine(pos_flat, h, meta_w, y_sorted, s_len)


def kernel(x_prompt, x_sample, cache_k, cache_v, cache_mem_k, cache_mem_v, page_table, mem_prompt, rel_bias, norm1_g, w_in, q_norm_g, k_norm_g, gmlp_ln_g, gmlp_ln_b, gmlp_ws, gmlp_bs, mem_norm_g, w_mem_kv, mq_norm_g, mk_norm_g, w_br_moba, w_br_gmlp, w_br_mem, w_out, norm2_g, w_router, b_router, w_gate_up, b_gate_up, w_down, b_down):
    batch, s_len, d = x_prompt.shape
    n_seq, dec_len, _ = x_sample.shape
    depth, n_phys, page, heads, hd = cache_k.shape
    _, _, mem, mem_heads, mem_hd = cache_mem_k.shape
    assert depth == 1 and batch == 1 and dec_len == 1
    moba_w, mem_w, gw = heads * hd, mem_heads * mem_hd, gmlp_ln_g.shape[-1]
    n_pages = page_table.shape[1]
    past_len = n_pages * page
    assert MOBA_BLOCK % page == 0 and past_len % MOBA_BLOCK == 0 and s_len % MOBA_BLOCK == 0
    ppb = MOBA_BLOCK // page
    nb = past_len // MOBA_BLOCK
    assert nb >= MOBA_TOPK
    n = s_len + n_seq

    x_all = jnp.concatenate([x_prompt.reshape(s_len, d), x_sample.reshape(n_seq, d)], axis=0)
    w_in2 = w_in.reshape(d, -1)
    xn = _rmsnorm(x_all, norm1_g, BF16)
    off = 0
    q = _proj(xn, w_in2, off, moba_w, "headnorm", q_norm_g, hd); off += moba_w
    k = _proj(xn, w_in2, off, moba_w, "headnorm", k_norm_g, hd); off += moba_w
    v = _proj(xn, w_in2, off, moba_w); off += moba_w
    z = _proj(xn, w_in2, off, 2 * gw, "gelu"); off += 2 * gw
    mq = _proj(xn, w_in2, off, mem_w, "headnorm", mq_norm_g, mem_hd); off += mem_w
    gates = _proj(xn, w_in2, off, 3 * d, "sigmoid")

    bias_tab = _bias_tables(rel_bias)
    att_p = _moba_prompt(q, k, v, bias_tab, s_len, heads, hd)
    q_s3, k_s3, v_s3 = (a[s_len:].reshape(n_seq, 1, moba_w) for a in (q, k, v))
    cache_k4 = cache_k.reshape(n_phys, page, heads, hd)
    cache_v4 = cache_v.reshape(n_phys, page, heads, hd)
    pt_flat = page_table.reshape(-1)
    sel = _sample_gate(pt_flat, q_s3.reshape(n_seq, heads, hd), cache_k4, n_seq, nb, ppb)
    sel_flat = jnp.swapaxes(sel[:, :MOBA_TOPK, :heads], 1, 2).reshape(-1)
    sample_bias = jnp.stack([bias_tab[:, 1, 0, :],
                             jnp.broadcast_to(bias_tab[:, 1, MOBA_BLOCK - 1, 0:1], (heads, MOBA_BLOCK)),
                             jnp.broadcast_to(bias_tab[:, 0, 0, 0:1], (heads, MOBA_BLOCK))], axis=1)
    att_s = _moba_sample(pt_flat, sel_flat, q_s3, k_s3, v_s3, sample_bias, cache_k4, cache_v4, n_seq, nb, ppb)
    att = jnp.concatenate([att_p, att_s.reshape(n_seq, moba_w)], axis=0)

    g_p, g_s, gv_s = _gmlp(z, s_len, gmlp_ln_g, gmlp_ln_b, gmlp_ws[0], gmlp_bs[0])
    g_out = jnp.concatenate([g_p, g_s], axis=0)

    mem_n = _rmsnorm(mem_prompt.reshape(mem, d), mem_norm_g, BF16)
    w_mem2 = w_mem_kv.reshape(d, 2 * mem_w)
    mk_p = _proj(mem_n, w_mem2, 0, mem_w, "headnorm", mk_norm_g, mem_hd)
    mv_p = _proj(mem_n, w_mem2, mem_w, mem_w)
    m_p = _mem_attn_prompt(mq, mk_p, mv_p, s_len, mem_heads)
    m_s = _mem_attn_sample(mq[s_len:].reshape(n_seq, 1, mem_w), cache_mem_k.reshape(n_seq, mem, mem_w),
                           cache_mem_v.reshape(n_seq, mem, mem_w), mem_heads)
    m_out = jnp.concatenate([m_p, m_s.reshape(n_seq, mem_w)], axis=0)

    mixed = _merge(att, g_out, m_out, gates, w_br_moba.reshape(moba_w, d), w_br_gmlp.reshape(gw, d), w_br_mem.reshape(mem_w, d))
    h = _proj(mixed, w_out.reshape(d, d), 0, d, "residual", x_all)
    n_experts = w_router.shape[-1]
    out_p, out_s = _moe(h, s_len, norm2_g, w_router.reshape(d, n_experts), b_router.reshape(n_experts),
                        w_gate_up.reshape(n_experts, d, -1), b_gate_up.reshape(n_experts, -1),
                        w_down.reshape(n_experts, -1, d), b_down.reshape(n_experts, d))

    y_prompt = out_p.reshape(batch, s_len, d)
    y_sample = out_s.reshape(n_seq, dec_len, d)
    k_prompt = k[:s_len].reshape(1, batch, s_len, heads, hd)
    v_prompt = v[:s_len].reshape(1, batch, s_len, heads, hd)
    mem_k_prompt = mk_p.reshape(1, batch, mem, mem_heads, mem_hd)
    mem_v_prompt = mv_p.reshape(1, batch, mem, mem_heads, mem_hd)
    k_sample = k[s_len:].reshape(1, n_seq, dec_len, heads, hd)
    v_sample = v[s_len:].reshape(1, n_seq, dec_len, heads, hd)
    gmlp_v_sample = gv_s.reshape(1, n_seq, dec_len, gmlp_ws.shape[1], -1)
    return (y_prompt, y_sample, k_prompt, v_prompt, mem_k_prompt, mem_v_prompt, k_sample, v_sample, gmlp_v_sample)
```

```python
import functools
import math

import numpy as np
import jax
import jax.numpy as jnp
from jax import lax
from jax.experimental import pallas as pl
from jax.experimental.pallas import tpu as pltpu

MOBA_BLOCK = 256
MOBA_TOPK = 3
REL_BUCKETS = 32
REL_MAX_DISTANCE = 128
TOP_K = 4
SWIGLU_LIMIT = 7.0
SWIGLU_ALPHA = 1.702
NORM_EPS = 1e-6

V7X_VMEM_BYTES = 64 * 1024 * 1024
VMEM_LIMIT = V7X_VMEM_BYTES - 12 * 1024 * 1024
LANES = 128

ROW_TILE = 1040
COL_TILE = 256
PROJ_COL_TILE = 512
NORM_ROWS = 256
MOBA_SUPER = 4
GATE_BLOCKS = 4
MOE_CHUNK = 512
MOE_TAIL = 128
MOE_COL_TILE = 512
MOE_VISIT_ROWS = 1280
DISPATCH_ROWS = 256
COMBINE_TOKENS = 64

F32 = jnp.float32
BF16 = jnp.bfloat16
HI = lax.Precision.HIGHEST
NT = (((1,), (1,)), ((), ()))
NEG_INF = float("-inf")
MASK_BIG = 2.0 ** 100


def _params(sem, vmem=VMEM_LIMIT):
    return pltpu.CompilerParams(dimension_semantics=sem, vmem_limit_bytes=vmem)


def _tile(n, pref):
    return n if n <= pref else pref


def _rmsnorm_kernel(x_ref, g_ref, o_ref):
    x = x_ref[...]
    y = x * lax.rsqrt(jnp.mean(x * x, axis=-1, keepdims=True) + NORM_EPS)
    o_ref[...] = (y * g_ref[...]).astype(o_ref.dtype)


def _rmsnorm(x, g, out_dtype):
    n, d = x.shape
    tm = _tile(n, NORM_ROWS)
    return pl.pallas_call(
        _rmsnorm_kernel,
        grid=(pl.cdiv(n, tm),),
        in_specs=[pl.BlockSpec((tm, d), lambda i: (i, 0)), pl.BlockSpec((1, d), lambda i: (0, 0))],
        out_specs=pl.BlockSpec((tm, d), lambda i: (i, 0)),
        out_shape=jax.ShapeDtypeStruct((n, d), out_dtype),
        compiler_params=_params(("parallel",)),
        name="rmsnorm",
    )(x, g.reshape(1, d))


def _proj_kernel(*refs, epilogue, head_dim):
    x_ref, w_ref, o_ref = refs[0], refs[1], refs[-1]
    acc = jnp.dot(x_ref[...], w_ref[...], preferred_element_type=F32)
    if epilogue == "headnorm":
        g = refs[2][...]
        for s in range(acc.shape[1] // head_dim):
            a = acc[:, s * head_dim:(s + 1) * head_dim]
            y = a * lax.rsqrt(jnp.mean(a * a, axis=-1, keepdims=True) + NORM_EPS)
            o_ref[:, s * head_dim:(s + 1) * head_dim] = (y * g).astype(o_ref.dtype)
    elif epilogue == "gelu":
        o_ref[...] = (acc * (lax.erf(acc * (1.0 / math.sqrt(2.0))) + 1.0) * 0.5).astype(o_ref.dtype)
    elif epilogue == "sigmoid":
        o_ref[...] = jax.nn.sigmoid(acc).astype(o_ref.dtype)
    elif epilogue == "residual":
        o_ref[...] = (refs[2][...] + acc).astype(o_ref.dtype)
    else:
        o_ref[...] = acc.astype(o_ref.dtype)


def _proj(x, w, col_off, width, epilogue="none", extra=None, head_dim=0, out_dtype=F32):
    n, k = x.shape
    tm = _tile(n, ROW_TILE)
    tn = _tile(width, PROJ_COL_TILE)
    while width % tn or col_off % tn:
        tn //= 2
    assert tn % LANES == 0
    joff = col_off // tn
    in_specs = [pl.BlockSpec((tm, k), lambda i, j: (i, 0)), pl.BlockSpec((k, tn), lambda i, j: (0, joff + j))]
    args = [x, w]
    if epilogue == "headnorm":
        assert tn % head_dim == 0
        in_specs.append(pl.BlockSpec((1, head_dim), lambda i, j: (0, 0)))
        args.append(extra.reshape(1, head_dim))
    elif epilogue == "residual":
        in_specs.append(pl.BlockSpec((tm, tn), lambda i, j: (i, j)))
        args.append(extra)
    return pl.pallas_call(
        functools.partial(_proj_kernel, epilogue=epilogue, head_dim=head_dim),
        grid=(pl.cdiv(n, tm), width // tn),
        in_specs=in_specs,
        out_specs=pl.BlockSpec((tm, tn), lambda i, j: (i, j)),
        out_shape=jax.ShapeDtypeStruct((n, width), out_dtype),
        compiler_params=_params(("parallel", "arbitrary")),
        name="proj_" + epilogue,
    )(*args)


def _t5_bucket_np(dist):
    n = np.maximum(dist, 0)
    max_exact = REL_BUCKETS // 2
    log_ratio = (np.log(np.maximum(n, 1).astype(np.float32) / np.float32(max_exact))
                 / np.float32(math.log(REL_MAX_DISTANCE / max_exact)))
    large = max_exact + (log_ratio * np.float32(REL_BUCKETS - max_exact)).astype(np.int32)
    return np.where(n < max_exact, n, np.minimum(large, REL_BUCKETS - 1)).astype(np.int32)


def _bias_table_kernel(rel_ref, bucket_ref, o_ref):
    h = pl.program_id(0)
    for t in range(2):
        bucket = bucket_ref[t]
        acc = jnp.zeros(bucket.shape, F32)
        for b in range(REL_BUCKETS):
            acc = jnp.where(bucket == b, rel_ref[b, h], acc)
        o_ref[0, t] = acc - rel_ref[REL_BUCKETS - 1, h]


def _bias_tables(rel_bias):
    assert MOBA_BLOCK >= REL_MAX_DISTANCE
    heads = rel_bias.shape[1]
    i = np.arange(MOBA_BLOCK)[:, None]
    j = np.arange(MOBA_BLOCK)[None, :]
    buckets = jnp.asarray(np.stack([_t5_bucket_np(i - j), _t5_bucket_np(MOBA_BLOCK + i - j)]))
    return pl.pallas_call(
        _bias_table_kernel,
        grid=(heads,),
        in_specs=[pl.BlockSpec(memory_space=pltpu.SMEM),
                  pl.BlockSpec((2, MOBA_BLOCK, MOBA_BLOCK), lambda h: (0, 0, 0))],
        out_specs=pl.BlockSpec((1, 2, MOBA_BLOCK, MOBA_BLOCK), lambda h: (h, 0, 0, 0)),
        out_shape=jax.ShapeDtypeStruct((heads, 2, MOBA_BLOCK, MOBA_BLOCK), F32),
        compiler_params=_params(("arbitrary",)),
        name="bias_tables",
    )(rel_bias, buckets)


def _top_blocks(gate, blk_iota, n_blocks, axis):
    picked = []
    mask = jnp.zeros(gate.shape, jnp.bool_)
    g = gate
    for _ in range(MOBA_TOPK):
        m = jnp.max(g, axis=axis, keepdims=True)
        first = jnp.min(jnp.where((g == m) & (m > NEG_INF), blk_iota, n_blocks), axis=axis, keepdims=True)
        pick = blk_iota == first
        picked.append(first)
        mask = mask | pick
        g = jnp.where(pick, NEG_INF, g)
    return picked, mask


def _moba_prompt_kernel(q_ref, k_ref, v_ref, bias_ref, o_ref, kmean_ref, kaug_ref, vbf_ref, s_ref, *, scale):
    h, qb = pl.program_id(0), pl.program_id(1)
    blk = MOBA_BLOCK
    s_len, hd = k_ref.shape
    nb = s_len // blk

    @pl.when((h == 0) & (qb == 0))
    def _():
        key_blk = lax.broadcasted_iota(jnp.int32, (s_len, LANES), 0) // blk
        lane = lax.broadcasted_iota(jnp.int32, (s_len, LANES), 1)
        kaug_ref[:, hd:] = jnp.where(key_blk == lane, -MASK_BIG, 0.0).astype(BF16)

    @pl.when(qb == 0)
    def _():
        kmean_ref[...] = jnp.zeros(kmean_ref.shape, F32)
        kmean_ref[:nb, :] = jnp.mean(k_ref[...].reshape(nb, blk, hd), axis=1)
        kaug_ref[:, :hd] = k_ref[...].astype(BF16)
        vbf_ref[...] = v_ref[...].astype(BF16)

    q = q_ref[...]
    gate_t = lax.dot_general(kmean_ref[...], q, NT, precision=HI, preferred_element_type=F32)
    blk_iota_t = lax.broadcasted_iota(jnp.int32, (LANES, blk), 0)
    _, sel_t = _top_blocks(jnp.where(blk_iota_t < qb, gate_t, NEG_INF), blk_iota_t, LANES, axis=0)
    sel_f = jnp.transpose(sel_t.astype(F32))
    blk_iota = lax.broadcasted_iota(jnp.int32, (blk, LANES), 1)

    qs = (q * scale).astype(BF16)
    row = lax.broadcasted_iota(jnp.int32, (blk, blk), 0)
    col = lax.broadcasted_iota(jnp.int32, (blk, blk), 1)

    def logits(start, rows):
        return lax.dot_general(qs, kaug_ref[pl.ds(start, rows), :hd], NT, preferred_element_type=F32)

    def pv(p, start, rows):
        return jnp.dot(p.astype(BF16), vbf_ref[pl.ds(start, rows), :], preferred_element_type=F32)

    def chose(kb):
        return jnp.max(jnp.where(blk_iota == kb, sel_f, 0.0), axis=1, keepdims=True) > 0.0

    def update(carry, s, start, rows):
        m, l, acc = carry
        m_new = jnp.maximum(m, jnp.max(s, axis=1, keepdims=True))
        alpha = jnp.exp(m - m_new)
        p = jnp.exp(s - m_new)
        return m_new, alpha * l + jnp.sum(p, axis=1, keepdims=True), alpha * acc + pv(p, start, rows)

    first = qb == 0
    prev_blk = jnp.maximum(qb - 1, 0)
    near = pl.multiple_of(prev_blk * blk, blk)
    raw = logits(near, 2 * blk)
    causal = jnp.where(row >= col, 0.0, NEG_INF)
    picked_prev = jnp.where(chose(prev_blk), 0.0, NEG_INF)
    s_lo = raw[:, :blk] + jnp.where(first, bias_ref[0, 0] + causal, bias_ref[0, 1] + picked_prev)
    s_hi = raw[:, blk:] + jnp.where(first, NEG_INF, bias_ref[0, 0] + causal)
    s = jnp.concatenate([s_lo, s_hi], axis=1)
    m = jnp.max(s, axis=1, keepdims=True)
    p = jnp.exp(s - m)
    carry = (m, jnp.sum(p, axis=1, keepdims=True), pv(p, near, 2 * blk))

    blocked = jnp.where((sel_f > 0.0) & (blk_iota < qb - 1), 0.0, 1.0)
    q_aug = jnp.concatenate([qs, blocked.astype(BF16)], axis=1)
    span = MOBA_SUPER * blk

    def masked_logits(g):
        start = pl.multiple_of(g * span, span)
        return lax.dot_general(q_aug, kaug_ref[pl.ds(start, span), :], NT, preferred_element_type=F32)

    def body(g, carry):
        nxt = masked_logits(g + 1)
        carry = update(carry, s_ref[...], pl.multiple_of(g * span, span), span)
        s_ref[...] = nxt
        return carry

    last = jnp.maximum((jnp.maximum(qb - 1, 0) + MOBA_SUPER - 1) // MOBA_SUPER - 1, 0)
    s_ref[...] = masked_logits(0)
    carry = lax.fori_loop(0, last, body, carry)
    m, l, acc = update(carry, s_ref[...], pl.multiple_of(last * span, span), span)
    o_ref[...] = (acc / l).astype(o_ref.dtype)


def _moba_prompt(q, k, v, bias_tab, s_len, heads, hd):
    assert s_len % (MOBA_SUPER * MOBA_BLOCK) == 0 and s_len // MOBA_BLOCK <= LANES
    nqb = s_len // MOBA_BLOCK
    return pl.pallas_call(
        functools.partial(_moba_prompt_kernel, scale=hd ** -0.5),
        grid=(heads, nqb),
        in_specs=[pl.BlockSpec((MOBA_BLOCK, hd), lambda h, i: (i, h)),
                  pl.BlockSpec((s_len, hd), lambda h, i: (0, h)),
                  pl.BlockSpec((s_len, hd), lambda h, i: (0, h)),
                  pl.BlockSpec((1, 2, MOBA_BLOCK, MOBA_BLOCK), lambda h, i: (h, 0, 0, 0))],
        out_specs=pl.BlockSpec((MOBA_BLOCK, hd), lambda h, i: (i, h)),
        out_shape=jax.ShapeDtypeStruct((s_len, heads * hd), BF16),
        scratch_shapes=[pltpu.VMEM((LANES, hd), F32), pltpu.VMEM((s_len, hd + LANES), BF16),
                        pltpu.VMEM((s_len, hd), BF16), pltpu.VMEM((MOBA_BLOCK, MOBA_SUPER * MOBA_BLOCK), F32)],
        compiler_params=_params(("arbitrary", "arbitrary")),
        name="moba_prompt",
    )(q, k, v, bias_tab)


def _sample_gate_kernel(pt_ref, q_ref, *refs, ppb, bps):
    page_refs, sel_ref, kmean_ref = refs[:bps * ppb], refs[bps * ppb], refs[bps * ppb + 1]
    j = pl.program_id(1)
    nb, heads, _ = kmean_ref.shape
    for t in range(bps):
        total = jnp.sum(page_refs[t * ppb][0], axis=0)
        for p in range(1, ppb):
            total = total + jnp.sum(page_refs[t * ppb + p][0], axis=0)
        kmean_ref[j * bps + t] = total * (1.0 / MOBA_BLOCK)

    @pl.when(j == pl.num_programs(1) - 1)
    def _():
        gate = jnp.sum(kmean_ref[...] * q_ref[...], axis=-1)
        blk_iota = lax.broadcasted_iota(jnp.int32, (nb, heads), 0)
        picked, _ = _top_blocks(gate, blk_iota, nb, axis=0)
        sel_ref[...] = jnp.zeros(sel_ref.shape, jnp.int32)
        for t in range(MOBA_TOPK):
            sel_ref[0, t:t + 1, :heads] = picked[t]


def _sample_gate(page_table_flat, q_s, cache_k, n_seq, nb, ppb):
    _, page, heads, hd = cache_k.shape
    bps = _tile(nb, GATE_BLOCKS)
    assert nb % bps == 0 and heads <= LANES and MOBA_TOPK <= 8
    n_pages = nb * ppb
    page_specs = [pl.BlockSpec((1, page, heads, hd),
                               lambda b, j, pt, i=i: (pt[b * n_pages + j * bps * ppb + i], 0, 0, 0))
                  for i in range(bps * ppb)]
    return pl.pallas_call(
        functools.partial(_sample_gate_kernel, ppb=ppb, bps=bps),
        grid_spec=pltpu.PrefetchScalarGridSpec(
            num_scalar_prefetch=1,
            grid=(n_seq, nb // bps),
            in_specs=[pl.BlockSpec((1, heads, hd), lambda b, j, pt: (b, 0, 0))] + page_specs,
            out_specs=pl.BlockSpec((1, 8, LANES), lambda b, j, pt: (b, 0, 0)),
            scratch_shapes=[pltpu.VMEM((nb, heads, hd), F32)]),
        out_shape=jax.ShapeDtypeStruct((n_seq, 8, LANES), jnp.int32),
        compiler_params=_params(("parallel", "arbitrary")),
        name="moba_sample_gate",
    )(page_table_flat, q_s, *([cache_k] * (bps * ppb)))


def _moba_sample_kernel(pt_ref, sel_ref, q_ref, kn_ref, vn_ref, sb_ref, ck_ref, cv_ref, o_ref, kbuf, vbuf, sems,
                        *, nb, ppb, scale):
    b = pl.program_id(0)
    _, heads, n_keys, hd = kbuf.shape
    page = n_keys // (MOBA_TOPK * ppb)
    n_pages = nb * ppb
    pages = [(h, t, p) for h in range(heads) for t in range(MOBA_TOPK) for p in range(ppb)]

    def page_copies(seq, h, t, p):
        half = seq % 2
        phys = pt_ref[seq * n_pages + sel_ref[(seq * heads + h) * MOBA_TOPK + t] * ppb + p]
        dst = pl.ds((t * ppb + p) * page, page)
        return (pltpu.make_async_copy(ck_ref.at[phys, :, h, :], kbuf.at[half, h, dst, :], sems.at[half, 0]),
                pltpu.make_async_copy(cv_ref.at[phys, :, h, :], vbuf.at[half, h, dst, :], sems.at[half, 1]))

    def fetch(seq):
        for page_id in pages:
            for copy in page_copies(seq, *page_id):
                copy.start()

    @pl.when(b == 0)
    def _():
        fetch(b)

    @pl.when(b + 1 < pl.num_programs(0))
    def _():
        fetch(b + 1)

    for page_id in pages:
        for copy in page_copies(b, *page_id):
            copy.wait()

    cur = b % 2
    for h in range(heads):
        cols = slice(h * hd, (h + 1) * hd)
        q = q_ref[0, :, cols]
        q8 = jnp.broadcast_to(q, (8, hd)).astype(BF16)
        near, far, self_bias = sb_ref[h, 0:1, :], sb_ref[h, 1:2, :], sb_ref[h, 2:3, 0:1]
        bias = jnp.concatenate(
            [jnp.where(sel_ref[(b * heads + h) * MOBA_TOPK + t] == nb - 1, near, far) for t in range(MOBA_TOPK)], axis=1)
        s = lax.dot_general(q8, kbuf[cur, h].astype(BF16), NT, preferred_element_type=F32)[0:1] * scale + bias
        s_self = jnp.sum(q * kn_ref[0, :, cols], axis=1, keepdims=True) * scale + self_bias
        m = jnp.maximum(jnp.max(s, axis=1, keepdims=True), s_self)
        p = jnp.exp(s - m)
        p_self = jnp.exp(s_self - m)
        l = jnp.sum(p, axis=1, keepdims=True) + p_self
        p8 = jnp.broadcast_to(p, (8, n_keys)).astype(BF16)
        acc = jnp.dot(p8, vbuf[cur, h].astype(BF16), preferred_element_type=F32)[0:1] + p_self * vn_ref[0, :, cols]
        o_ref[0, :, cols] = (acc / l).astype(o_ref.dtype)


def _moba_sample(page_table_flat, sel_flat, q_s3, k_s3, v_s3, sample_bias, cache_k, cache_v, n_seq, nb, ppb):
    _, page, heads, hd = cache_k.shape
    n_keys = MOBA_TOPK * ppb * page
    row_spec = pl.BlockSpec((1, 1, heads * hd), lambda b, pt, sel: (b, 0, 0))
    return pl.pallas_call(
        functools.partial(_moba_sample_kernel, nb=nb, ppb=ppb, scale=hd ** -0.5),
        grid_spec=pltpu.PrefetchScalarGridSpec(
            num_scalar_prefetch=2,
            grid=(n_seq,),
            in_specs=[row_spec, row_spec, row_spec,
                      pl.BlockSpec((heads, 3, MOBA_BLOCK), lambda b, pt, sel: (0, 0, 0)),
                      pl.BlockSpec(memory_space=pl.ANY), pl.BlockSpec(memory_space=pl.ANY)],
            out_specs=row_spec,
            scratch_shapes=[pltpu.VMEM((2, heads, n_keys, hd), F32), pltpu.VMEM((2, heads, n_keys, hd), F32),
                            pltpu.SemaphoreType.DMA((2, 2))]),
        out_shape=jax.ShapeDtypeStruct((n_seq, 1, heads * hd), BF16),
        compiler_params=_params(("arbitrary",)),
        name="moba_sample",
    )(page_table_flat, sel_flat, q_s3, k_s3, v_s3, sample_bias, cache_k, cache_v)


def _layernorm(x, g, b):
    mu = jnp.mean(x, axis=-1, keepdims=True)
    var = jnp.mean(jnp.square(x - mu), axis=-1, keepdims=True)
    return (x - mu) * lax.rsqrt(var + NORM_EPS) * g + b


def _gmlp_prompt_kernel(z_ref, g_ref, b_ref, ws_ref, bs_ref, o_ref):
    groups, chunk, _ = ws_ref.shape
    gw = g_ref.shape[1]
    gd = gw // groups
    u = z_ref[:, :gw]
    gv = _layernorm(z_ref[:, gw:], g_ref[...], b_ref[...]).astype(BF16)
    row = lax.broadcasted_iota(jnp.int32, (chunk, chunk), 0)
    col = lax.broadcasted_iota(jnp.int32, (chunk, chunk), 1)
    for g in range(groups):
        ws = jnp.where(row >= col, ws_ref[g], 0.0).astype(BF16)
        mixed = jnp.dot(ws, gv[:, g * gd:(g + 1) * gd], preferred_element_type=F32) + bs_ref[:, g:g + 1]
        o_ref[:, g * gd:(g + 1) * gd] = (u[:, g * gd:(g + 1) * gd] * mixed).astype(o_ref.dtype)


def _gmlp_sample_kernel(z_ref, g_ref, b_ref, w0_ref, b0_ref, o_ref, gv_ref):
    gw = g_ref.shape[1]
    gv = _layernorm(z_ref[:, gw:], g_ref[...], b_ref[...])
    gv_ref[...] = gv
    o_ref[...] = (z_ref[:, :gw] * (w0_ref[...] * gv + b0_ref[...])).astype(o_ref.dtype)


def _gmlp(z, s_len, ln_g, ln_b, ws, bs):
    n, two_gw = z.shape
    gw = two_gw // 2
    groups, chunk, _ = ws.shape
    gd = gw // groups
    assert s_len % chunk == 0
    g2, b2 = ln_g.reshape(1, gw), ln_b.reshape(1, gw)
    out_p = pl.pallas_call(
        _gmlp_prompt_kernel,
        grid=(s_len // chunk,),
        in_specs=[pl.BlockSpec((chunk, two_gw), lambda c: (c, 0)),
                  pl.BlockSpec((1, gw), lambda c: (0, 0)), pl.BlockSpec((1, gw), lambda c: (0, 0)),
                  pl.BlockSpec((groups, chunk, chunk), lambda c: (0, 0, 0)),
                  pl.BlockSpec((chunk, groups), lambda c: (0, 0))],
        out_specs=pl.BlockSpec((chunk, gw), lambda c: (c, 0)),
        out_shape=jax.ShapeDtypeStruct((s_len, gw), BF16),
        compiler_params=_params(("parallel",)),
        name="gmlp_prompt",
    )(z, g2, b2, ws, bs.T)
    z_s = z[s_len:]
    n_s = n - s_len
    w0 = jnp.repeat(ws[:, 0, 0], gd).reshape(1, gw)
    b0 = jnp.repeat(bs[:, 0], gd).reshape(1, gw)
    out_s, gv_s = pl.pallas_call(
        _gmlp_sample_kernel,
        out_shape=(jax.ShapeDtypeStruct((n_s, gw), BF16), jax.ShapeDtypeStruct((n_s, gw), F32)),
        name="gmlp_sample",
    )(z_s, g2, b2, w0, b0)
    return out_p, out_s, gv_s


def _mem_attn_kernel(q_ref, k_ref, v_ref, o_ref, *, heads, scale, batched):
    q = q_ref[0] if batched else q_ref[...]
    k = k_ref[0] if batched else k_ref[...]
    v = v_ref[0] if batched else v_ref[...]
    rows = max(q.shape[0], 8)
    hd = q.shape[1] // heads
    for h in range(heads):
        sl = slice(h * hd, (h + 1) * hd)
        qh = jnp.broadcast_to(q[:, sl], (rows, hd)).astype(BF16)
        s = lax.dot_general(qh, k[:, sl].astype(BF16), NT, preferred_element_type=F32) * scale
        p = jnp.exp(s - jnp.max(s, axis=1, keepdims=True))
        l = jnp.sum(p, axis=1, keepdims=True)
        out = jnp.dot(p.astype(BF16), v[:, sl].astype(BF16), preferred_element_type=F32) / l
        out = out[:q.shape[0]].astype(o_ref.dtype)
        if batched:
            o_ref[0, :, sl] = out
        else:
            o_ref[:, sl] = out


def _mem_attn_prompt(mq, mk, mv, s_len, heads):
    width = mq.shape[1]
    mem = mk.shape[0]
    tq = _tile(s_len, 512)
    return pl.pallas_call(
        functools.partial(_mem_attn_kernel, heads=heads, scale=(width // heads) ** -0.5, batched=False),
        grid=(s_len // tq,),
        in_specs=[pl.BlockSpec((tq, width), lambda i: (i, 0)),
                  pl.BlockSpec((mem, width), lambda i: (0, 0)), pl.BlockSpec((mem, width), lambda i: (0, 0))],
        out_specs=pl.BlockSpec((tq, width), lambda i: (i, 0)),
        out_shape=jax.ShapeDtypeStruct((s_len, width), BF16),
        compiler_params=_params(("parallel",)),
        name="mem_attn_prompt",
    )(mq, mk, mv)


def _mem_attn_sample(mq_s3, ck, cv, heads):
    n_seq, mem, width = ck.shape
    return pl.pallas_call(
        functools.partial(_mem_attn_kernel, heads=heads, scale=(width // heads) ** -0.5, batched=True),
        grid=(n_seq,),
        in_specs=[pl.BlockSpec((1, 1, width), lambda b: (b, 0, 0)),
                  pl.BlockSpec((1, mem, width), lambda b: (b, 0, 0)), pl.BlockSpec((1, mem, width), lambda b: (b, 0, 0))],
        out_specs=pl.BlockSpec((1, 1, width), lambda b: (b, 0, 0)),
        out_shape=jax.ShapeDtypeStruct((n_seq, 1, width), BF16),
        compiler_params=_params(("parallel",)),
        name="mem_attn_sample",
    )(mq_s3, ck, cv)


def _merge_kernel(a_ref, g_ref, m_ref, ga_ref, gg_ref, gm_ref, wa_ref, wg_ref, wm_ref, o_ref):
    mixed = ga_ref[...] * jnp.dot(a_ref[...], wa_ref[...], preferred_element_type=F32)
    mixed = mixed + gg_ref[...] * jnp.dot(g_ref[...], wg_ref[...], preferred_element_type=F32)
    mixed = mixed + gm_ref[...] * jnp.dot(m_ref[...], wm_ref[...], preferred_element_type=F32)
    o_ref[...] = mixed.astype(o_ref.dtype)


def _merge(att, gout, mout, gates, w_a, w_g, w_m):
    n = att.shape[0]
    d = w_a.shape[1]
    tm = _tile(n, ROW_TILE)
    tn = _tile(d, COL_TILE)
    nj = d // tn

    def act_spec(a):
        return pl.BlockSpec((tm, a.shape[1]), lambda i, j: (i, 0))

    def gate_spec(branch):
        return pl.BlockSpec((tm, tn), lambda i, j: (i, branch * nj + j))

    def w_spec(w):
        return pl.BlockSpec((w.shape[0], tn), lambda i, j: (0, j))

    return pl.pallas_call(
        _merge_kernel,
        grid=(pl.cdiv(n, tm), nj),
        in_specs=[act_spec(att), act_spec(gout), act_spec(mout), gate_spec(0), gate_spec(1), gate_spec(2),
                  w_spec(w_a), w_spec(w_g), w_spec(w_m)],
        out_specs=pl.BlockSpec((tm, tn), lambda i, j: (i, j)),
        out_shape=jax.ShapeDtypeStruct((n, d), BF16),
        compiler_params=_params(("parallel", "arbitrary")),
        name="merge",
    )(att, gout, mout, gates, gates, gates, w_a, w_g, w_m)


def _router_kernel(h_ref, g_ref, wr_ref, br_ref, hn_ref, mi_ref, mw_ref, cnt_ref, carry_ref, *, n_tokens, n_experts):
    i = pl.program_id(0)
    tm = h_ref.shape[0]

    @pl.when(i == 0)
    def _():
        carry_ref[...] = jnp.zeros(carry_ref.shape, F32)

    x = h_ref[...]
    hn = x * lax.rsqrt(jnp.mean(x * x, axis=-1, keepdims=True) + NORM_EPS) * g_ref[...]
    hn_ref[...] = hn
    logits = jnp.dot(hn, wr_ref[...], precision=HI, preferred_element_type=F32) + br_ref[...]
    e_iota = lax.broadcasted_iota(jnp.int32, (tm, n_experts), 1)
    vals, ids = [], []
    l = logits
    for _ in range(TOP_K):
        m = jnp.max(l, axis=1, keepdims=True)
        first = jnp.min(jnp.where(l == m, e_iota, n_experts), axis=1, keepdims=True)
        vals.append(m)
        ids.append(first)
        l = jnp.where(e_iota == first, NEG_INF, l)
    ex = [jnp.exp(v - vals[0]) for v in vals]
    denom = ex[0]
    for e in ex[1:]:
        denom = denom + e
    valid = (i * tm + lax.broadcasted_iota(jnp.int32, (tm, 1), 0)) < n_tokens
    multi_hot = jnp.zeros((tm, n_experts), F32)
    for k in range(TOP_K):
        multi_hot = multi_hot + (e_iota == ids[k]).astype(F32)
    multi_hot = jnp.where(valid, multi_hot, 0.0)
    lower = (lax.broadcasted_iota(jnp.int32, (tm, tm), 0) > lax.broadcasted_iota(jnp.int32, (tm, tm), 1)).astype(BF16)
    before = jnp.dot(lower, multi_hot.astype(BF16), preferred_element_type=F32) + carry_ref[:, :n_experts]
    lane = lax.broadcasted_iota(jnp.int32, (tm, LANES), 1)
    meta_i = jnp.zeros((tm, LANES), jnp.int32)
    meta_w = jnp.zeros((tm, LANES), F32)
    for k in range(TOP_K):
        rank = jnp.sum(jnp.where(e_iota == ids[k], before, 0.0), axis=1, keepdims=True).astype(jnp.int32)
        meta_i = jnp.where(lane == k, ids[k], meta_i)
        meta_i = jnp.where(lane == TOP_K + k, rank, meta_i)
        meta_w = jnp.where(lane == k, ex[k] / denom, meta_w)
    mi_ref[...] = meta_i
    mw_ref[...] = meta_w
    carry_ref[:, :n_experts] = carry_ref[:, :n_experts] + jnp.sum(multi_hot, axis=0, keepdims=True)
    cnt_ref[...] = carry_ref[...].astype(jnp.int32)


def _router(h, g, w_router, b_router):
    n, d = h.shape
    n_experts = w_router.shape[1]
    assert n_experts <= LANES
    tm = _tile(n, NORM_ROWS)
    return pl.pallas_call(
        functools.partial(_router_kernel, n_tokens=n, n_experts=n_experts),
        grid=(pl.cdiv(n, tm),),
        in_specs=[pl.BlockSpec((tm, d), lambda i: (i, 0)), pl.BlockSpec((1, d), lambda i: (0, 0)),
                  pl.BlockSpec((d, n_experts), lambda i: (0, 0)), pl.BlockSpec((1, n_experts), lambda i: (0, 0))],
        out_specs=[pl.BlockSpec((tm, d), lambda i: (i, 0)), pl.BlockSpec((tm, LANES), lambda i: (i, 0)),
                   pl.BlockSpec((tm, LANES), lambda i: (i, 0)), pl.BlockSpec((1, LANES), lambda i: (0, 0))],
        out_shape=(jax.ShapeDtypeStruct((n, d), F32), jax.ShapeDtypeStruct((n, LANES), jnp.int32),
                   jax.ShapeDtypeStruct((n, LANES), F32), jax.ShapeDtypeStruct((1, LANES), jnp.int32)),
        scratch_shapes=[pltpu.VMEM((1, LANES), F32)],
        compiler_params=_params(("arbitrary",)),
        name="router",
    )(h, g.reshape(1, d), w_router, b_router.reshape(1, n_experts))


def _dispatch_kernel(valid_ref, src_ref, hn_ref, o_ref, buf_ref, sem):
    c = pl.program_id(0)
    count = valid_ref[c]

    def row_copy(r, src_row):
        return pltpu.make_async_copy(hn_ref.at[pl.ds(src_row, 1)], buf_ref.at[pl.ds(r, 1)], sem)

    @pl.when(c == 0)
    def _():
        buf_ref[...] = jnp.zeros(buf_ref.shape, buf_ref.dtype)

    @pl.when(count > 0)
    def _():
        def start_pair(i, carry):
            for lane in range(2):
                r = 2 * i + lane
                row_copy(r, src_ref[r]).start(priority=lane)
            return carry

        def wait(r, carry):
            row_copy(r, 0).wait()
            return carry

        lax.fori_loop(0, count // 2, start_pair, 0)

        @pl.when(count % 2 == 1)
        def _():
            row_copy(count - 1, src_ref[count - 1]).start()

        lax.fori_loop(0, count, wait, 0)
        o_ref[...] = buf_ref[...].astype(o_ref.dtype)

    @pl.when(count == 0)
    def _():
        o_ref[...] = jnp.zeros(o_ref.shape, o_ref.dtype)


def _dispatch(chunk_valid, src_rows, hn, n_rows):
    d = hn.shape[1]
    return pl.pallas_call(
        _dispatch_kernel,
        grid_spec=pltpu.PrefetchScalarGridSpec(
            num_scalar_prefetch=1,
            grid=(n_rows // DISPATCH_ROWS,),
            in_specs=[pl.BlockSpec((DISPATCH_ROWS,), lambda c, valid: (c,), memory_space=pltpu.SMEM),
                      pl.BlockSpec(memory_space=pl.ANY)],
            out_specs=pl.BlockSpec((DISPATCH_ROWS, d), lambda c, valid: (c, 0)),
            scratch_shapes=[pltpu.VMEM((DISPATCH_ROWS, d), F32), pltpu.SemaphoreType.DMA(())]),
        out_shape=jax.ShapeDtypeStruct((n_rows, d), BF16),
        compiler_params=_params(("arbitrary",)),
        name="moe_dispatch",
    )(chunk_valid, src_rows, hn)


def _visit_specs(n_col_tiles):
    def col(v, j, vr):
        return jnp.where(vr[v] > 0, j, n_col_tiles - 1)

    def rows(v, j, ve, vb, vr):
        return (vb[v], 0)

    def weight(v, j, ve, vb, vr):
        return (ve[v], 0, col(v, j, vr))

    def out(v, j, ve, vb, vr):
        return (v, j)

    return rows, weight, out


def _run_visit(n_rows, o_ref, compute, finish, stage_ref):
    units_per_chunk = MOE_CHUNK // MOE_TAIL
    full = n_rows // MOE_CHUNK
    tail_units = (n_rows - full * MOE_CHUNK + MOE_TAIL - 1) // MOE_TAIL
    merge = (full > 0) & (tail_units == 1)
    full = jnp.where(merge, full - 1, full)
    tail_units = jnp.where(merge, units_per_chunk + 1, tail_units)

    @pl.when((pl.program_id(0) == 0) & (pl.program_id(1) == 0))
    def _():
        stage_ref[...] = jnp.zeros(stage_ref.shape, stage_ref.dtype)

    @pl.when(full > 0)
    def _():
        stage_ref[...] = compute(0, MOE_CHUNK)

        def body(c, carry):
            r0 = pl.multiple_of(c * MOE_CHUNK, MOE_CHUNK)
            nxt = compute(r0, MOE_CHUNK)
            finish(stage_ref[...], pl.multiple_of(r0 - MOE_CHUNK, MOE_CHUNK), MOE_CHUNK)
            stage_ref[...] = nxt
            return carry

        lax.fori_loop(1, full, body, 0)

    staged_r = pl.multiple_of(jnp.maximum(full - 1, 0) * MOE_CHUNK, MOE_CHUNK)
    r_full = pl.multiple_of(full * MOE_CHUNK, MOE_CHUNK)
    for units in range(1, units_per_chunk + 2):
        @pl.when(tail_units == units)
        def _(rows=units * MOE_TAIL):
            tail = compute(r_full, rows)
            finish(stage_ref[...], staged_r, MOE_CHUNK)
            finish(tail, r_full, rows)

    @pl.when((tail_units == 0) & (full > 0))
    def _():
        finish(stage_ref[...], staged_r, MOE_CHUNK)

    def zero(c, carry):
        o_ref[pl.ds(pl.multiple_of(c * MOE_TAIL, MOE_TAIL), MOE_TAIL), :] = jnp.zeros((MOE_TAIL, o_ref.shape[1]), o_ref.dtype)
        return carry

    lax.fori_loop(full * units_per_chunk + tail_units, o_ref.shape[0] // MOE_TAIL, zero, 0)


def _gate_up_kernel(ve_ref, vb_ref, vr_ref, x_ref, w_ref, b_ref, p_ref, o_ref, wb_ref, stage_ref):
    @pl.when(vr_ref[pl.program_id(0)] > 0)
    def _():
        wb_ref[...] = w_ref[0].astype(BF16)

    tn = wb_ref.shape[1]

    def compute(r0, rows):
        return jnp.dot(x_ref[pl.ds(r0, rows), :], wb_ref[...], preferred_element_type=F32) + b_ref[0]

    def finish(gu, r0, rows):
        nxt = pltpu.roll(gu, tn - 1, 1)
        glu = jnp.minimum(gu, SWIGLU_LIMIT)
        lin = jnp.clip(nxt, -SWIGLU_LIMIT, SWIGLU_LIMIT)
        even = (lax.broadcasted_iota(jnp.int32, (rows, tn), 1) % 2) == 0
        act = jnp.where(even, glu * jax.nn.sigmoid(SWIGLU_ALPHA * glu) * (lin + 1.0), 0.0)
        o_ref[pl.ds(r0, rows), :] = jnp.dot(act.astype(BF16), p_ref[...], preferred_element_type=F32).astype(o_ref.dtype)

    _run_visit(vr_ref[pl.program_id(0)], o_ref, compute, finish, stage_ref)


def _gate_up(vis_expert, vis_block, vis_rows, xs, w_gate_up, b_gate_up):
    n_rows, d = xs.shape
    n_experts, _, two_f = w_gate_up.shape
    tn = _tile(two_f, MOE_COL_TILE)
    sel = np.zeros((tn, tn // 2), np.float32)
    sel[2 * np.arange(tn // 2), np.arange(tn // 2)] = 1.0
    rows, weight, out = _visit_specs(two_f // tn)
    return pl.pallas_call(
        _gate_up_kernel,
        grid_spec=pltpu.PrefetchScalarGridSpec(
            num_scalar_prefetch=3,
            grid=(n_rows // MOE_VISIT_ROWS, two_f // tn),
            in_specs=[pl.BlockSpec((MOE_VISIT_ROWS, d), rows),
                      pl.BlockSpec((1, d, tn), weight),
                      pl.BlockSpec((1, 1, tn), weight),
                      pl.BlockSpec((tn, tn // 2), lambda v, j, ve, vb, vr: (0, 0))],
            out_specs=pl.BlockSpec((MOE_VISIT_ROWS, tn // 2), out),
            scratch_shapes=[pltpu.VMEM((d, tn), BF16), pltpu.VMEM((MOE_CHUNK, tn), F32)]),
        out_shape=jax.ShapeDtypeStruct((n_rows, two_f // 2), BF16),
        compiler_params=_params(("arbitrary", "arbitrary")),
        name="moe_gate_up",
    )(vis_expert, vis_block, vis_rows, xs, w_gate_up, b_gate_up.reshape(n_experts, 1, two_f), jnp.asarray(sel, BF16))


def _down_kernel(ve_ref, vb_ref, vr_ref, x_ref, w_ref, b_ref, o_ref, wb_ref, stage_ref):
    @pl.when(vr_ref[pl.program_id(0)] > 0)
    def _():
        wb_ref[...] = w_ref[0].astype(BF16)

    def compute(r0, rows):
        return jnp.dot(x_ref[pl.ds(r0, rows), :], wb_ref[...], preferred_element_type=F32) + b_ref[0]

    def finish(y, r0, rows):
        o_ref[pl.ds(r0, rows), :] = y

    _run_visit(vr_ref[pl.program_id(0)], o_ref, compute, finish, stage_ref)


def _down(vis_expert, vis_block, vis_rows, act, w_down, b_down):
    n_rows, f = act.shape
    n_experts, _, d = w_down.shape
    tn = _tile(d, MOE_COL_TILE)
    rows, weight, out = _visit_specs(d // tn)
    return pl.pallas_call(
        _down_kernel,
        grid_spec=pltpu.PrefetchScalarGridSpec(
            num_scalar_prefetch=3,
            grid=(n_rows // MOE_VISIT_ROWS, d // tn),
            in_specs=[pl.BlockSpec((MOE_VISIT_ROWS, f), rows),
                      pl.BlockSpec((1, f, tn), weight),
                      pl.BlockSpec((1, 1, tn), weight)],
            out_specs=pl.BlockSpec((MOE_VISIT_ROWS, tn), out),
            scratch_shapes=[pltpu.VMEM((f, tn), BF16), pltpu.VMEM((MOE_CHUNK, tn), F32)]),
        out_shape=jax.ShapeDtypeStruct((n_rows, d), F32),
        compiler_params=_params(("arbitrary", "arbitrary")),
        name="moe_down",
    )(vis_expert, vis_block, vis_rows, act, w_down, b_down.reshape(n_experts, 1, d))


def _combine_kernel(pos_ref, h_ref, w_ref, y_ref, op_ref, os_ref, buf_ref, sem):
    tt = h_ref.shape[0]

    def row_copy(t, k, src_row):
        return pltpu.make_async_copy(y_ref.at[pl.ds(src_row, 1)], buf_ref.at[pl.ds(k * tt + t, 1)], sem)

    def start(t, carry):
        for k in range(TOP_K):
            row_copy(t, k, pos_ref[t * TOP_K + k]).start(priority=k % 2)
        return carry

    def wait(t, carry):
        for k in range(TOP_K):
            row_copy(t, k, 0).wait()
        return carry

    lax.fori_loop(0, tt, start, 0)
    lax.fori_loop(0, tt, wait, 0)
    out = h_ref[...]
    for k in range(TOP_K):
        out = out + w_ref[:, k:k + 1] * buf_ref[k * tt:(k + 1) * tt, :]
    is_prompt = pl.program_id(0) < pl.num_programs(0) - 1

    @pl.when(is_prompt)
    def _():
        op_ref[...] = out

    @pl.when(jnp.logical_not(is_prompt))
    def _():
        os_ref[...] = out[:os_ref.shape[0]]


def _combine(pos_flat, h, weights, y_sorted, s_len):
    n, d = h.shape
    tt = _tile(s_len, COMBINE_TOKENS)
    n_s = n - s_len
    assert s_len % tt == 0 and 0 < n_s <= tt
    n_p_tiles = s_len // tt
    pos_flat = jnp.pad(pos_flat, (0, (n_p_tiles + 1) * tt * TOP_K - pos_flat.shape[0]))
    return pl.pallas_call(
        _combine_kernel,
        grid=(n_p_tiles + 1,),
        in_specs=[pl.BlockSpec((tt * TOP_K,), lambda i: (i,), memory_space=pltpu.SMEM),
                  pl.BlockSpec((tt, d), lambda i: (i, 0)),
                  pl.BlockSpec((tt, LANES), lambda i: (i, 0)),
                  pl.BlockSpec(memory_space=pl.ANY)],
        out_specs=[pl.BlockSpec((tt, d), lambda i: (jnp.minimum(i, n_p_tiles - 1), 0)),
                   pl.BlockSpec((n_s, d), lambda i: (0, 0))],
        out_shape=(jax.ShapeDtypeStruct((s_len, d), F32), jax.ShapeDtypeStruct((n_s, d), F32)),
        scratch_shapes=[pltpu.VMEM((TOP_K * tt, d), F32), pltpu.SemaphoreType.DMA(())],
        compiler_params=_params(("arbitrary",)),
        name="moe_combine",
    )(pos_flat, h, weights, y_sorted)


def _moe(h, s_len, norm2_g, w_router, b_router, w_gate_up, b_gate_up, w_down, b_down):
    n, d = h.shape
    n_experts = w_router.shape[1]
    hn, meta_i, meta_w, counts = _router(h, norm2_g, w_router, b_router)
    ids, rank = meta_i[:, :TOP_K], meta_i[:, TOP_K:2 * TOP_K]
    counts = counts[0, :n_experts]

    max_visits = (n * TOP_K) // MOE_VISIT_ROWS + n_experts
    n_rows = max_visits * MOE_VISIT_ROWS
    visits_per = (counts + MOE_VISIT_ROWS - 1) // MOE_VISIT_ROWS
    visit_end = jnp.cumsum(visits_per)
    visit_start = visit_end - visits_per
    total = visit_end[-1]
    v_eff = jnp.minimum(jnp.arange(max_visits, dtype=jnp.int32), total - 1)
    vis_expert = jnp.minimum(jnp.searchsorted(visit_end, v_eff, side="right"), n_experts - 1).astype(jnp.int32)
    vis_rows = jnp.clip(counts[vis_expert] - (v_eff - visit_start[vis_expert]) * MOE_VISIT_ROWS, 0, MOE_VISIT_ROWS)
    vis_rows = jnp.where(jnp.arange(max_visits) < total, vis_rows, 0).astype(jnp.int32)
    pos = (visit_start[ids] * MOE_VISIT_ROWS + rank).astype(jnp.int32)
    pos_flat = pos.reshape(-1)
    token = jnp.broadcast_to(jnp.arange(n, dtype=jnp.int32)[:, None], (n, TOP_K)).reshape(-1)
    src_rows = jnp.zeros((n_rows,), jnp.int32).at[pos_flat].set(token)
    chunks_per_visit = MOE_VISIT_ROWS // DISPATCH_ROWS
    chunk_valid = jnp.clip(jnp.repeat(vis_rows, chunks_per_visit)
                           - jnp.tile(jnp.arange(chunks_per_visit, dtype=jnp.int32) * DISPATCH_ROWS, max_visits),
                           0, DISPATCH_ROWS).astype(jnp.int32)

    xs = _dispatch(chunk_valid, src_rows, hn, n_rows)
    act = _gate_up(vis_expert, v_eff, vis_rows, xs, w_gate_up, b_gate_up)
    y_sorted = _down(vis_expert, v_eff, vis_rows, act, w_down, b_down)
    return _combine(pos_flat, h, meta_w, y_sorted, s_len)


def kernel(x_prompt, x_sample, cache_k, cache_v, cache_mem_k, cache_mem_v, page_table, mem_prompt, rel_bias, norm1_g, w_in, q_norm_g, k_norm_g, gmlp_ln_g, gmlp_ln_b, gmlp_ws, gmlp_bs, mem_norm_g, w_mem_kv, mq_norm_g, mk_norm_g, w_br_moba, w_br_gmlp, w_br_mem, w_out, norm2_g, w_router, b_router, w_gate_up, b_gate_up, w_down, b_down):
    batch, s_len, d = x_prompt.shape
    n_seq, dec_len, _ = x_sample.shape
    depth, n_phys, page, heads, hd = cache_k.shape
    _, _, mem, mem_heads, mem_hd = cache_mem_k.shape
    assert depth == 1 and batch == 1 and dec_len == 1
    moba_w, mem_w, gw = heads * hd, mem_heads * mem_hd, gmlp_ln_g.shape[-1]
    n_pages = page_table.shape[1]
    past_len = n_pages * page
    assert MOBA_BLOCK % page == 0 and past_len % MOBA_BLOCK == 0 and s_len % MOBA_BLOCK == 0
    ppb = MOBA_BLOCK // page
    nb = past_len // MOBA_BLOCK
    assert nb >= MOBA_TOPK
    n = s_len + n_seq

    x_all = jnp.concatenate([x_prompt.reshape(s_len, d), x_sample.reshape(n_seq, d)], axis=0)
    w_in2 = w_in.reshape(d, -1)
    xn = _rmsnorm(x_all, norm1_g, BF16)
    off = 0
    q = _proj(xn, w_in2, off, moba_w, "headnorm", q_norm_g, hd); off += moba_w
    k = _proj(xn, w_in2, off, moba_w, "headnorm", k_norm_g, hd); off += moba_w
    v = _proj(xn, w_in2, off, moba_w); off += moba_w
    z = _proj(xn, w_in2, off, 2 * gw, "gelu"); off += 2 * gw
    mq = _proj(xn, w_in2, off, mem_w, "headnorm", mq_norm_g, mem_hd); off += mem_w
    gates = _proj(xn, w_in2, off, 3 * d, "sigmoid")

    bias_tab = _bias_tables(rel_bias)
    att_p = _moba_prompt(q, k, v, bias_tab, s_len, heads, hd)
    q_s3, k_s3, v_s3 = (a[s_len:].reshape(n_seq, 1, moba_w) for a in (q, k, v))
    cache_k4 = cache_k.reshape(n_phys, page, heads, hd)
    cache_v4 = cache_v.reshape(n_phys, page, heads, hd)
    pt_flat = page_table.reshape(-1)
    sel = _sample_gate(pt_flat, q_s3.reshape(n_seq, heads, hd), cache_k4, n_seq, nb, ppb)
    sel_flat = jnp.swapaxes(sel[:, :MOBA_TOPK, :heads], 1, 2).reshape(-1)
    sample_bias = jnp.stack([bias_tab[:, 1, 0, :],
                             jnp.broadcast_to(bias_tab[:, 1, MOBA_BLOCK - 1, 0:1], (heads, MOBA_BLOCK)),
                             jnp.broadcast_to(bias_tab[:, 0, 0, 0:1], (heads, MOBA_BLOCK))], axis=1)
    att_s = _moba_sample(pt_flat, sel_flat, q_s3, k_s3, v_s3, sample_bias, cache_k4, cache_v4, n_seq, nb, ppb)
    att = jnp.concatenate([att_p, att_s.reshape(n_seq, moba_w)], axis=0)

    g_p, g_s, gv_s = _gmlp(z, s_len, gmlp_ln_g, gmlp_ln_b, gmlp_ws[0], gmlp_bs[0])
    g_out = jnp.concatenate([g_p, g_s], axis=0)

    mem_n = _rmsnorm(mem_prompt.reshape(mem, d), mem_norm_g, BF16)
    w_mem2 = w_mem_kv.reshape(d, 2 * mem_w)
    mk_p = _proj(mem_n, w_mem2, 0, mem_w, "headnorm", mk_norm_g, mem_hd)
    mv_p = _proj(mem_n, w_mem2, mem_w, mem_w)
    m_p = _mem_attn_prompt(mq, mk_p, mv_p, s_len, mem_heads)
    m_s = _mem_attn_sample(mq[s_len:].reshape(n_seq, 1, mem_w), cache_mem_k.reshape(n_seq, mem, mem_w),
                           cache_mem_v.reshape(n_seq, mem, mem_w), mem_heads)
    m_out = jnp.concatenate([m_p, m_s.reshape(n_seq, mem_w)], axis=0)

    mixed = _merge(att, g_out, m_out, gates, w_br_moba.reshape(moba_w, d), w_br_gmlp.reshape(gw, d), w_br_mem.reshape(mem_w, d))
    h = _proj(mixed, w_out.reshape(d, d), 0, d, "residual", x_all)
    n_experts = w_router.shape[-1]
    out_p, out_s = _moe(h, s_len, norm2_g, w_router.reshape(d, n_experts), b_router.reshape(n_experts),
                        w_gate_up.reshape(n_experts, d, -1), b_gate_up.reshape(n_experts, -1),
                        w_down.reshape(n_experts, -1, d), b_down.reshape(n_experts, d))

    y_prompt = out_p.reshape(batch, s_len, d)
    y_sample = out_s.reshape(n_seq, dec_len, d)
    k_prompt = k[:s_len].reshape(1, batch, s_len, heads, hd)
    v_prompt = v[:s_len].reshape(1, batch, s_len, heads, hd)
    mem_k_prompt = mk_p.reshape(1, batch, mem, mem_heads, mem_hd)
    mem_v_prompt = mv_p.reshape(1, batch, mem, mem_heads, mem_hd)
    k_sample = k[s_len:].reshape(1, n_seq, dec_len, heads, hd)
    v_sample = v[s_len:].reshape(1, n_seq, dec_len, heads, hd)
    gmlp_v_sample = gv_s.reshape(1, n_seq, dec_len, gmlp_ws.shape[1], -1)
    return (y_prompt, y_sample, k_prompt, v_prompt, mem_k_prompt, mem_v_prompt, k_sample, v_sample, gmlp_v_sample)
```
